```python
import jax, jax.numpy as jnp
from jax import lax
import numpy as np

D_MODEL = 1024
BATCH = 4
SEQ = 4096
DEPTH = 2

N_CONV_LAYERS = DEPTH // 2
N_ATTN_LAYERS = DEPTH - N_CONV_LAYERS
CONV_WIDTH = 3
HEAD_DIM = 64
N_HEADS = D_MODEL // HEAD_DIM
BRANCHES = ((128, 1), (512, 4), (2048, 16))
N_BRANCHES = len(BRANCHES)
Q_WIDTH = N_BRANCHES * N_HEADS * HEAD_DIM
D_FF = -(-8 * D_MODEL // (3 * 256)) * 256
ROPE_THETA = 10000.0
RMS_EPS = 1e-6
NEG_INF = -1e30

kernel_name = "yoco_shortconv_dilated_attention_trunk"


def rms_norm(x, g):
    xf = x.astype(jnp.float32)
    y = xf * lax.rsqrt(jnp.mean(xf * xf, axis=-1, keepdims=True) + RMS_EPS)
    return (y * g.astype(jnp.float32)).astype(x.dtype)


def rope(t, positions):
    half = HEAD_DIM // 2
    inv_freq = ROPE_THETA ** (-jnp.arange(half, dtype=jnp.float32) / half)
    ang = positions.astype(jnp.float32)[..., None] * inv_freq
    cos = jnp.cos(ang)[:, :, None, :]
    sin = jnp.sin(ang)[:, :, None, :]
    t1 = t[..., :half].astype(jnp.float32)
    t2 = t[..., half:].astype(jnp.float32)
    out = jnp.concatenate([t1 * cos - t2 * sin, t2 * cos + t1 * sin], axis=-1)
    return out.astype(t.dtype)


def short_conv_mixer(x, w_in, conv_w, w_out):
    b_gate, c_gate, h = jnp.split(x @ w_in, 3, axis=-1)
    u = c_gate * h
    rhs = conv_w[:, None, :].astype(u.dtype)
    conv = lax.conv_general_dilated(
        u, rhs, window_strides=(1,), padding=[(CONV_WIDTH - 1, 0)],
        dimension_numbers=("NWC", "WIO", "NWC"), feature_group_count=u.shape[-1])
    return (b_gate * conv) @ w_out


def swiglu(x, w_gate_up, w_down):
    g, u = jnp.split(x @ w_gate_up, 2, axis=-1)
    return (jax.nn.silu(g) * u) @ w_down


def dilated_branch(q, k, v, window, dilation):
    band = window // dilation
    B, S, H, Dh = q.shape
    chunk = dilation * band
    Sp = -(-S // chunk) * chunk
    nb = Sp // chunk
    pad = ((0, 0), (0, Sp - S), (0, 0), (0, 0))

    def to_blocks(t):
        t = jnp.pad(t, pad).reshape(B, nb, band, dilation, H, Dh)
        return t.transpose(0, 3, 4, 1, 2, 5)

    def with_prev(t):
        prev = jnp.pad(t, ((0, 0), (0, 0), (0, 0), (1, 0), (0, 0), (0, 0)))[:, :, :, :-1]
        return jnp.concatenate([prev, t], axis=4)

    qb = to_blocks(q * (HEAD_DIM ** -0.5))
    kk = with_prev(to_blocks(k))
    vv = with_prev(to_blocks(v))
    s = jnp.einsum("brhnqd,brhnkd->brhnqk", qb, kk).astype(jnp.float32)
    qi = jnp.arange(band)[:, None]
    kj = jnp.arange(2 * band)[None, :]
    dist = qi + band - kj
    in_band = (dist >= 0) & (dist <= band)
    has_prev = (kj >= band)[None] | (jnp.arange(nb)[:, None, None] > 0)
    mask = in_band[None] & has_prev
    s = jnp.where(mask, s, NEG_INF)
    m = jnp.max(s, axis=-1)
    p = jnp.exp(s - m[..., None])
    l = jnp.sum(p, axis=-1)
    o = jnp.einsum("brhnqk,brhnkd->brhnqd", p, vv.astype(jnp.float32)) / l[..., None]
    lse = m + jnp.log(l)
    o = o.transpose(0, 3, 4, 1, 2, 5).reshape(B, Sp, H, Dh)[:, :S]
    lse = lse.transpose(0, 3, 4, 1, 2).reshape(B, Sp, H)[:, :S]
    return o, lse


def dilated_attention_mixer(x, positions, k_sh, v_sh, w_q, w_o):
    B, S, _ = x.shape
    q = (x @ w_q).reshape(B, S, N_BRANCHES * N_HEADS, HEAD_DIM)
    q = rope(q, positions).reshape(B, S, N_BRANCHES, N_HEADS, HEAD_DIM)
    outs, lses = [], []
    for g, (window, dilation) in enumerate(BRANCHES):
        o, lse = dilated_branch(q[:, :, g], k_sh[:, :, g], v_sh[:, :, g], window, dilation)
        outs.append(o)
        lses.append(lse)
    wts = jax.nn.softmax(jnp.stack(lses, axis=0), axis=0)
    o = jnp.einsum("gbsh,gbshd->bshd", wts, jnp.stack(outs, axis=0))
    return o.astype(x.dtype).reshape(B, S, N_HEADS * HEAD_DIM) @ w_o


def shared_kv(h, positions, kv_norm, w_kv):
    B, S, _ = h.shape
    kv = (rms_norm(h, kv_norm) @ w_kv).reshape(B, S, 2, N_BRANCHES * N_HEADS, HEAD_DIM)
    k = rope(kv[:, :, 0], positions).reshape(B, S, N_BRANCHES, N_HEADS, HEAD_DIM)
    v = kv[:, :, 1].reshape(B, S, N_BRANCHES, N_HEADS, HEAD_DIM)
    return k, v


def setup_inputs(seed: int = 0) -> dict:
    key = jax.random.key(seed)
    ks = jax.random.split(key, 20)
    f32 = jnp.float32

    def w(k, shape, fan_in):
        return jax.random.normal(k, shape, f32) * (fan_in ** -0.5)

    def gain(k, shape):
        return 1.0 + 0.05 * jax.random.normal(k, shape, f32)

    x = jax.random.normal(ks[0], (BATCH, SEQ, D_MODEL), f32)
    offset = jax.random.randint(ks[1], (BATCH, 1), 0, 4096, dtype=jnp.int32)
    positions = offset + jnp.arange(SEQ, dtype=jnp.int32)[None, :]
    nA, nB = N_CONV_LAYERS, N_ATTN_LAYERS
    return {
        "x": x,
        "positions": positions,
        "mix_norm_pre": gain(ks[2], (DEPTH, D_MODEL)),
        "mix_norm_post": gain(ks[3], (DEPTH, D_MODEL)),
        "ffn_norm_pre": gain(ks[4], (DEPTH, D_MODEL)),
        "ffn_norm_post": gain(ks[5], (DEPTH, D_MODEL)),
        "ffn_w_gate_up": w(ks[6], (DEPTH, D_MODEL, 2 * D_FF), D_MODEL),
        "ffn_w_down": w(ks[7], (DEPTH, D_FF, D_MODEL), D_FF),
        "conv_w_in": w(ks[8], (nA, D_MODEL, 3 * D_MODEL), D_MODEL),
        "conv_w": w(ks[9], (nA, CONV_WIDTH, D_MODEL), CONV_WIDTH),
        "conv_w_out": w(ks[10], (nA, D_MODEL, D_MODEL), D_MODEL),
        "kv_norm": gain(ks[11], (D_MODEL,)),
        "w_kv": w(ks[12], (D_MODEL, 2 * Q_WIDTH), D_MODEL),
        "w_q": w(ks[13], (nB, D_MODEL, Q_WIDTH), D_MODEL),
        "w_o": w(ks[14], (nB, N_HEADS * HEAD_DIM, D_MODEL), N_HEADS * HEAD_DIM),
    }


def reference(x, positions, mix_norm_pre, mix_norm_post, ffn_norm_pre, ffn_norm_post,
              ffn_w_gate_up, ffn_w_down, conv_w_in, conv_w, conv_w_out,
              kv_norm, w_kv, w_q, w_o):
    h = x
    for layer in range(DEPTH):
        if layer == N_CONV_LAYERS:
            k_sh, v_sh = shared_kv(h, positions, kv_norm, w_kv)
        hn = rms_norm(h, mix_norm_pre[layer])
        if layer < N_CONV_LAYERS:
            y = short_conv_mixer(hn, conv_w_in[layer], conv_w[layer], conv_w_out[layer])
        else:
            j = layer - N_CONV_LAYERS
            y = dilated_attention_mixer(hn, positions, k_sh, v_sh, w_q[j], w_o[j])
        h = h + rms_norm(y, mix_norm_post[layer])
        f = swiglu(rms_norm(h, ffn_norm_pre[layer]), ffn_w_gate_up[layer], ffn_w_down[layer])
        h = h + rms_norm(f, ffn_norm_post[layer])
    return h
```

```python
import functools

import jax
import jax.numpy as jnp
from jax import lax
from jax.experimental import pallas as pl
from jax.experimental.pallas import tpu as pltpu

D_MODEL = 1024
HEAD_DIM = 64
N_HEADS = D_MODEL // HEAD_DIM
BRANCHES = ((128, 1), (512, 4), (2048, 16))
N_BRANCHES = len(BRANCHES)
Q_WIDTH = N_BRANCHES * N_HEADS * HEAD_DIM
CONV_WIDTH = 3
ROPE_THETA = 10000.0
RMS_EPS = 1e-6
NEG_INF = -1e30

LANES = 128
MXU_COLS = 256
BAND = 128
HEADS_PER_VREG = LANES // HEAD_DIM
N_HEAD_PAIRS = N_HEADS // HEADS_PER_VREG
VMEM_LIMIT_BYTES = 56 * 1024 * 1024

TOKEN_TILE = 512
ATTN_TILE = 256
CARRY_ROWS = 8

F32 = jnp.float32
BF16 = jnp.bfloat16


def _dot(a, b):
    return jnp.dot(a, b, preferred_element_type=F32)


def _rms_unit(x):
    return x * lax.rsqrt(jnp.mean(x * x, axis=-1, keepdims=True) + RMS_EPS)


def _resident(shape):
    return pl.BlockSpec(shape, lambda *_: (0,) * len(shape), pipeline_mode=pl.Buffered(1))


def _params(*semantics):
    return pltpu.CompilerParams(dimension_semantics=semantics, vmem_limit_bytes=VMEM_LIMIT_BYTES)


def _conv_mixer_kernel(x_ref, gpre_ref, win_ref, cw_ref, wout_ref, gpost_ref, o_ref, ubuf_ref):
    j = pl.program_id(1)
    tm = x_ref.shape[0]
    x = x_ref[...]
    hn = (_rms_unit(x) * gpre_ref[...]).astype(BF16)
    b_gate = _dot(hn, win_ref[:, 0:D_MODEL])
    u = _dot(hn, win_ref[:, D_MODEL:2 * D_MODEL]) * _dot(hn, win_ref[:, 2 * D_MODEL:3 * D_MODEL])

    @pl.when(j == 0)
    def _():
        ubuf_ref[0:CARRY_ROWS, :] = jnp.zeros((CARRY_ROWS, D_MODEL), F32)

    @pl.when(j > 0)
    def _():
        ubuf_ref[0:CARRY_ROWS, :] = ubuf_ref[tm:tm + CARRY_ROWS, :]

    ubuf_ref[CARRY_ROWS:CARRY_ROWS + tm, :] = u
    u1 = ubuf_ref[CARRY_ROWS - 1:CARRY_ROWS - 1 + tm, :]
    u2 = ubuf_ref[CARRY_ROWS - 2:CARRY_ROWS - 2 + tm, :]
    conv = cw_ref[0:1, :] * u2 + cw_ref[1:2, :] * u1 + cw_ref[2:3, :] * u
    y = _dot((b_gate * conv).astype(BF16), wout_ref[...])
    o_ref[...] = x + _rms_unit(y) * gpost_ref[...]


def _conv_mixer(x, g_pre, w_in, conv_w, w_out, g_post):
    batch, seq, d = x.shape
    tm = TOKEN_TILE
    row = pl.BlockSpec((None, tm, d), lambda b, j: (b, j, 0))
    return pl.pallas_call(
        _conv_mixer_kernel,
        grid=(batch, seq // tm),
        in_specs=[row, _resident((1, d)), _resident(w_in.shape), _resident(conv_w.shape),
                  _resident(w_out.shape), _resident((1, d))],
        out_specs=row,
        out_shape=jax.ShapeDtypeStruct(x.shape, F32),
        scratch_shapes=[pltpu.VMEM((tm + CARRY_ROWS, d), F32)],
        compiler_params=_params("arbitrary", "arbitrary"),
        name="conv_mixer",
    )(x, g_pre, w_in, conv_w, w_out, g_post)


def _ffn_kernel(h_ref, gpre_ref, wgu_ref, wd_ref, gpost_ref, o_ref, act_ref):
    d_ff = wd_ref.shape[0]
    h = h_ref[...]
    hn = (_rms_unit(h) * gpre_ref[...]).astype(BF16)
    for c in range(0, d_ff, MXU_COLS):
        g = _dot(hn, wgu_ref[:, c:c + MXU_COLS])
        u = _dot(hn, wgu_ref[:, d_ff + c:d_ff + c + MXU_COLS])
        act_ref[:, c:c + MXU_COLS] = (g * jax.nn.sigmoid(g) * u).astype(BF16)
    f = _dot(act_ref[...], wd_ref[...])
    o_ref[...] = h + _rms_unit(f) * gpost_ref[...]


def _ffn(h, g_pre, w_gate_up, w_down, g_post):
    tokens, d = h.shape
    d_ff = w_down.shape[0]
    assert d_ff % MXU_COLS == 0
    tm = TOKEN_TILE
    row = pl.BlockSpec((tm, d), lambda i: (i, 0))
    return pl.pallas_call(
        _ffn_kernel,
        grid=(tokens // tm,),
        in_specs=[row, _resident((1, d)), _resident(w_gate_up.shape), _resident(w_down.shape),
                  _resident((1, d))],
        out_specs=row,
        out_shape=jax.ShapeDtypeStruct(h.shape, F32),
        scratch_shapes=[pltpu.VMEM((tm, d_ff), BF16)],
        compiler_params=_params("arbitrary"),
        name="ffn",
    )(h, g_pre, w_gate_up, w_down, g_post)


def _qkv_kernel(h_ref, pos_ref, invf_ref, gkv_ref, gq_ref, wkv_ref, wq_ref, q_ref, k_ref, v_ref):
    y = _rms_unit(h_ref[...])
    a_kv = (y * gkv_ref[...]).astype(BF16)
    a_q = (y * gq_ref[...]).astype(BF16)

    ang = pos_ref[...].astype(F32) * invf_ref[...]
    cos = jnp.cos(ang)
    sin = jnp.sin(ang)
    lane = lax.broadcasted_iota(jnp.int32, (1, LANES), 1)
    first_half = (lane % HEAD_DIM) < (HEAD_DIM // 2)
    sin_signed = jnp.where(first_half, -sin, sin)

    def rope(t):
        partner = jnp.where(first_half,
                            pltpu.roll(t, LANES - HEAD_DIM // 2, axis=1),
                            pltpu.roll(t, HEAD_DIM // 2, axis=1))
        return t * cos + partner * sin_signed

    scale = HEAD_DIM ** -0.5
    for c in range(0, Q_WIDTH, MXU_COLS):
        tq = _dot(a_q, wq_ref[:, c:c + MXU_COLS])
        tk = _dot(a_kv, wkv_ref[:, c:c + MXU_COLS])
        for s in range(0, MXU_COLS, LANES):
            q_ref[:, c + s:c + s + LANES] = (rope(tq[:, s:s + LANES]) * scale).astype(BF16)
            k_ref[:, c + s:c + s + LANES] = rope(tk[:, s:s + LANES]).astype(BF16)
        v_ref[:, c:c + MXU_COLS] = _dot(a_kv, wkv_ref[:, Q_WIDTH + c:Q_WIDTH + c + MXU_COLS]).astype(BF16)


def _qkv(h, pos, inv_freq, g_kv, g_q, w_kv, w_q):
    tokens, d = h.shape
    tm = TOKEN_TILE
    row = pl.BlockSpec((tm, d), lambda i: (i, 0))
    out_row = pl.BlockSpec((tm, Q_WIDTH), lambda i: (i, 0))
    out = jax.ShapeDtypeStruct((tokens, Q_WIDTH), BF16)
    return pl.pallas_call(
        _qkv_kernel,
        grid=(tokens // tm,),
        in_specs=[row, pl.BlockSpec((tm, 1), lambda i: (i, 0)), _resident((1, LANES)),
                  _resident((1, d)), _resident((1, d)), _resident(w_kv.shape), _resident(w_q.shape)],
        out_specs=[out_row, out_row, out_row],
        out_shape=[out, out, out],
        compiler_params=_params("arbitrary"),
        name="qkv_rope",
    )(h, pos, inv_freq, g_kv, g_q, w_kv, w_q)


def _attn_kernel(q_ref, kc_ref, kp_ref, vc_ref, vp_ref, o_ref, lse_ref):
    t = pl.program_id(2)
    tq = q_ref.shape[0]
    lane = lax.broadcasted_iota(jnp.int32, (1, LANES), 1)
    low_head = lane < HEAD_DIM
    qi = lax.broadcasted_iota(jnp.int32, (BAND, 2 * BAND), 0)
    kj = lax.broadcasted_iota(jnp.int32, (BAND, 2 * BAND), 1)
    dist = qi + BAND - kj
    in_band = (dist >= 0) & (dist <= BAND)
    first_mask = in_band & ((kj >= BAND) | (t > 0))
    nt = (((1,), (1,)), ((), ()))

    for i in range(tq // BAND):
        rows = slice(i * BAND, (i + 1) * BAND)
        mask = first_mask if i == 0 else in_band
        lse_tile = jnp.zeros((BAND, LANES), F32)
        for j in range(N_HEAD_PAIRS):
            cols = slice(j * LANES, (j + 1) * LANES)
            q2 = q_ref[rows, cols]
            if i == 0:
                k2 = jnp.concatenate([kp_ref[:, cols], kc_ref[0:BAND, cols]], axis=0)
                v2 = jnp.concatenate([vp_ref[:, cols], vc_ref[0:BAND, cols]], axis=0)
            else:
                k2 = kc_ref[(i - 1) * BAND:(i + 1) * BAND, cols]
                v2 = vc_ref[(i - 1) * BAND:(i + 1) * BAND, cols]
            outs = []
            for hh in range(HEADS_PER_VREG):
                mine = low_head if hh == 0 else jnp.logical_not(low_head)
                qh = jnp.where(mine, q2, jnp.zeros_like(q2))
                s = lax.dot_general(qh, k2, nt, preferred_element_type=F32)
                s = jnp.where(mask, s, NEG_INF)
                m = jnp.max(s, axis=-1, keepdims=True)
                p = jnp.exp(s - m)
                l = jnp.sum(p, axis=-1, keepdims=True)
                outs.append(_dot(p.astype(BF16), v2) / l)
                lse_tile = jnp.where(lane == j * HEADS_PER_VREG + hh, m + jnp.log(l), lse_tile)
            o_ref[rows, cols] = jnp.where(low_head, outs[0], outs[1]).astype(o_ref.dtype)
        lse_ref[rows, :] = lse_tile


def _attn_branch(q, k, v, branch, dilation):
    batch, seq, _ = q.shape
    sd = seq // dilation
    tq = ATTN_TILE
    assert sd % tq == 0 and tq % BAND == 0
    ratio = tq // BAND

    def view(a):
        return a.reshape(batch, sd, dilation * a.shape[-1])

    def col(r):
        return r * N_BRANCHES + branch

    cur = pl.BlockSpec((None, tq, D_MODEL), lambda b, r, t: (b, t, col(r)))
    prev = pl.BlockSpec((None, BAND, D_MODEL),
                        lambda b, r, t: (b, jnp.maximum(t * ratio - 1, 0), col(r)))
    o, lse = pl.pallas_call(
        _attn_kernel,
        grid=(batch, dilation, sd // tq),
        in_specs=[cur, cur, prev, cur, prev],
        out_specs=[pl.BlockSpec((None, tq, D_MODEL), lambda b, r, t: (b, t, r)),
                   pl.BlockSpec((None, tq, LANES), lambda b, r, t: (b, t, r))],
        out_shape=[jax.ShapeDtypeStruct((batch, sd, dilation * D_MODEL), BF16),
                   jax.ShapeDtypeStruct((batch, sd, dilation * LANES), F32)],
        compiler_params=_params("arbitrary", "arbitrary", "arbitrary"),
        name=f"attn_d{dilation}",
    )(view(q), view(k), view(k), view(v), view(v))
    return o.reshape(batch * seq, D_MODEL), lse.reshape(batch * seq, LANES)


def _mix_kernel(h_ref, o0_ref, o1_ref, o2_ref, l0_ref, l1_ref, l2_ref, wo_ref, gpost_ref, out_ref):
    lses = [l0_ref[...], l1_ref[...], l2_ref[...]]
    top = jnp.maximum(jnp.maximum(lses[0], lses[1]), lses[2])
    es = [jnp.exp(l - top) for l in lses]
    denom = es[0] + es[1] + es[2]
    k_idx = lax.broadcasted_iota(jnp.int32, (2 * LANES, D_MODEL), 0)
    c_idx = lax.broadcasted_iota(jnp.int32, (2 * LANES, D_MODEL), 1)
    expand = ((k_idx % LANES) == (c_idx // HEAD_DIM)).astype(BF16)
    mixed = None
    for e, o_ref in zip(es, (o0_ref, o1_ref, o2_ref)):
        w = e / denom
        w_hi = w.astype(BF16)
        w_lo = (w - w_hi.astype(F32)).astype(BF16)
        w_full = _dot(jnp.concatenate([w_hi, w_lo], axis=1), expand)
        term = w_full * o_ref[...].astype(F32)
        mixed = term if mixed is None else mixed + term
    y = _dot(mixed.astype(BF16), wo_ref[...])
    out_ref[...] = h_ref[...] + _rms_unit(y) * gpost_ref[...]


def _mix(h, outs, lses, w_o, g_post):
    tokens, d = h.shape
    tm = TOKEN_TILE
    row = pl.BlockSpec((tm, d), lambda i: (i, 0))
    lrow = pl.BlockSpec((tm, LANES), lambda i: (i, 0))
    return pl.pallas_call(
        _mix_kernel,
        grid=(tokens // tm,),
        in_specs=[row, row, row, row, lrow, lrow, lrow, _resident(w_o.shape), _resident((1, d))],
        out_specs=row,
        out_shape=jax.ShapeDtypeStruct(h.shape, F32),
        compiler_params=_params("arbitrary"),
        name="mix_wo",
    )(h, *outs, *lses, w_o, g_post)


def kernel(x, positions, mix_norm_pre, mix_norm_post, ffn_norm_pre, ffn_norm_post, ffn_w_gate_up,
           ffn_w_down, conv_w_in, conv_w, conv_w_out, kv_norm, w_kv, w_q, w_o):
    batch, seq, d = x.shape
    tokens = batch * seq

    def gain(g):
        return g.reshape(1, d).astype(F32)

    half = HEAD_DIM // 2
    inv_freq = ROPE_THETA ** (-jnp.arange(half, dtype=F32) / half)
    inv_freq = jnp.tile(inv_freq, LANES // half).reshape(1, LANES)
    pos = positions.reshape(tokens, 1)

    h = _conv_mixer(x, gain(mix_norm_pre[0]), conv_w_in[0].astype(BF16), conv_w[0],
                    conv_w_out[0].astype(BF16), gain(mix_norm_post[0]))
    h = h.reshape(tokens, d)
    h = _ffn(h, gain(ffn_norm_pre[0]), ffn_w_gate_up[0].astype(BF16), ffn_w_down[0].astype(BF16),
             gain(ffn_norm_post[0]))

    q, k, v = _qkv(h, pos, inv_freq, gain(kv_norm), gain(mix_norm_pre[1]), w_kv.astype(BF16),
                   w_q[0].astype(BF16))
    q, k, v = (a.reshape(batch, seq, Q_WIDTH) for a in (q, k, v))
    outs, lses = [], []
    for g, (window, dilation) in enumerate(BRANCHES):
        assert window // dilation == BAND
        o, lse = _attn_branch(q, k, v, g, dilation)
        outs.append(o)
        lses.append(lse)
    h = _mix(h, outs, lses, w_o[0].astype(BF16), gain(mix_norm_post[1]))
    h = _ffn(h, gain(ffn_norm_pre[1]), ffn_w_gate_up[1].astype(BF16), ffn_w_down[1].astype(BF16),
             gain(ffn_norm_post[1]))
    return h.reshape(batch, seq, d)
```

```python
import jax
import jax.numpy as jnp
from jax import lax
from jax.experimental import pallas as pl
from jax.experimental.pallas import tpu as pltpu

D_MODEL = 1024
HEAD_DIM = 64
N_HEADS = D_MODEL // HEAD_DIM
BRANCHES = ((128, 1), (512, 4), (2048, 16))
DILATIONS = tuple(d for _, d in BRANCHES)
N_BRANCHES = len(BRANCHES)
Q_WIDTH = N_BRANCHES * N_HEADS * HEAD_DIM
CONV_WIDTH = 3
ROPE_THETA = 10000.0
RMS_EPS = 1e-6
NEG_INF = -1e30

LANES = 128
MXU_COLS = 256
BAND = 128
CHUNK = BAND * max(DILATIONS)
BLOCKS_PER_CHUNK = CHUNK // BAND
HEADS_PER_VREG = LANES // HEAD_DIM
N_HEAD_PAIRS = N_HEADS // HEADS_PER_VREG
N_SLABS = D_MODEL // LANES
VMEM_LIMIT_BYTES = 56 * 1024 * 1024

TOKEN_TILE = 512
CARRY_ROWS = 8

F32 = jnp.float32
BF16 = jnp.bfloat16


def _dot(a, b):
    return jnp.dot(a, b, preferred_element_type=F32)


def _rms_unit(x):
    return x * lax.rsqrt(jnp.mean(x * x, axis=-1, keepdims=True) + RMS_EPS)


def _resident(shape):
    return pl.BlockSpec(shape, lambda *_: (0,) * len(shape), pipeline_mode=pl.Buffered(1))


def _params(*semantics):
    return pltpu.CompilerParams(dimension_semantics=semantics, vmem_limit_bytes=VMEM_LIMIT_BYTES)


def _conv_mixer_kernel(x_ref, gpre_ref, win_ref, cw_ref, wout_ref, gpost_ref, o_ref, ubuf_ref):
    j = pl.program_id(1)
    tm = x_ref.shape[0]
    x = x_ref[...]
    hn = (_rms_unit(x) * gpre_ref[...]).astype(BF16)
    b_gate = _dot(hn, win_ref[:, 0:D_MODEL])
    u = _dot(hn, win_ref[:, D_MODEL:2 * D_MODEL]) * _dot(hn, win_ref[:, 2 * D_MODEL:3 * D_MODEL])

    @pl.when(j == 0)
    def _():
        ubuf_ref[0:CARRY_ROWS, :] = jnp.zeros((CARRY_ROWS, D_MODEL), F32)

    @pl.when(j > 0)
    def _():
        ubuf_ref[0:CARRY_ROWS, :] = ubuf_ref[tm:tm + CARRY_ROWS, :]

    ubuf_ref[CARRY_ROWS:CARRY_ROWS + tm, :] = u
    u1 = ubuf_ref[CARRY_ROWS - 1:CARRY_ROWS - 1 + tm, :]
    u2 = ubuf_ref[CARRY_ROWS - 2:CARRY_ROWS - 2 + tm, :]
    conv = cw_ref[0:1, :] * u2 + cw_ref[1:2, :] * u1 + cw_ref[2:3, :] * u
    y = _dot((b_gate * conv).astype(BF16), wout_ref[...])
    o_ref[...] = x + _rms_unit(y) * gpost_ref[...]


def _conv_mixer(x, g_pre, w_in, conv_w, w_out, g_post):
    batch, seq, d = x.shape
    tm = TOKEN_TILE
    row = pl.BlockSpec((None, tm, d), lambda b, j: (b, j, 0))
    return pl.pallas_call(
        _conv_mixer_kernel,
        grid=(batch, seq // tm),
        in_specs=[row, _resident((1, d)), _resident(w_in.shape), _resident(conv_w.shape),
                  _resident(w_out.shape), _resident((1, d))],
        out_specs=row,
        out_shape=jax.ShapeDtypeStruct(x.shape, F32),
        scratch_shapes=[pltpu.VMEM((tm + CARRY_ROWS, d), F32)],
        compiler_params=_params("arbitrary", "arbitrary"),
        name="conv_mixer",
    )(x, g_pre, w_in, conv_w, w_out, g_post)


def _swiglu_residual(h, gpre_ref, wgu_ref, wd_ref, gpost_ref, act_ref):
    d_ff = wd_ref.shape[0]
    hn = (_rms_unit(h) * gpre_ref[...]).astype(BF16)
    for c in range(0, d_ff, MXU_COLS):
        g = _dot(hn, wgu_ref[:, c:c + MXU_COLS])
        u = _dot(hn, wgu_ref[:, d_ff + c:d_ff + c + MXU_COLS])
        act_ref[:, c:c + MXU_COLS] = (g * jax.nn.sigmoid(g) * u).astype(BF16)
    f = _dot(act_ref[...], wd_ref[...])
    return h + _rms_unit(f) * gpost_ref[...]


def _ffn_kernel(h_ref, gpre_ref, wgu_ref, wd_ref, gpost_ref, o_ref, act_ref):
    o_ref[...] = _swiglu_residual(h_ref[...], gpre_ref, wgu_ref, wd_ref, gpost_ref, act_ref)


def _attn_out_ffn_kernel(h_ref, a_ref, wo_ref, gmix_ref, gpre_ref, wgu_ref, wd_ref, gpost_ref,
                         o_ref, act_ref):
    h = h_ref[...] + _rms_unit(_dot(a_ref[...], wo_ref[...])) * gmix_ref[...]
    o_ref[...] = _swiglu_residual(h, gpre_ref, wgu_ref, wd_ref, gpost_ref, act_ref)


def _ffn(h, g_pre, w_gate_up, w_down, g_post, attn=None):
    tokens, d = h.shape
    d_ff = w_down.shape[0]
    assert d_ff % MXU_COLS == 0
    tm = TOKEN_TILE
    row = pl.BlockSpec((tm, d), lambda i: (i, 0))
    ffn_specs = [_resident((1, d)), _resident(w_gate_up.shape), _resident(w_down.shape),
                 _resident((1, d))]
    if attn is None:
        body, specs, args = _ffn_kernel, [row], (h,)
    else:
        a, w_o, g_mix = attn
        body = _attn_out_ffn_kernel
        specs = [row, row, _resident(w_o.shape), _resident((1, d))]
        args = (h, a, w_o, g_mix)
    return pl.pallas_call(
        body,
        grid=(tokens // tm,),
        in_specs=specs + ffn_specs,
        out_specs=row,
        out_shape=jax.ShapeDtypeStruct(h.shape, F32),
        scratch_shapes=[pltpu.VMEM((tm, d_ff), BF16)],
        compiler_params=_params("arbitrary"),
        name="ffn" if attn is None else "attn_out_ffn",
    )(*args, g_pre, w_gate_up, w_down, g_post)


def _qkv_kernel(h_ref, pos_ref, invf_ref, gkv_ref, gq_ref, wkv_ref, wq_ref, *rest):
    out_refs, (hbuf_ref, trig_ref) = rest[:3 * N_BRANCHES], rest[3 * N_BRANCHES:]
    tm = h_ref.shape[0]

    for s in range(N_SLABS):
        hbuf_ref[s] = h_ref[:, s * LANES:(s + 1) * LANES]
    ang = pos_ref[...].astype(F32) * invf_ref[...]
    trig_ref[0] = jnp.cos(ang)
    trig_ref[1] = jnp.sin(ang)

    lane = lax.broadcasted_iota(jnp.int32, (1, LANES), 1)
    first_half = (lane % HEAD_DIM) < (HEAD_DIM // 2)
    scale = HEAD_DIM ** -0.5

    for g, dil in enumerate(DILATIONS):
        rows = tm // dil

        def gather(read):
            if dil == 1:
                return read(slice(None))
            return jnp.concatenate([read(pl.ds(r, rows, stride=dil)) for r in range(dil)], axis=0)

        h = gather(lambda idx: jnp.concatenate([hbuf_ref[s, idx, :] for s in range(N_SLABS)], axis=1))
        cos = gather(lambda idx: trig_ref[0, idx, :])
        sin = gather(lambda idx: trig_ref[1, idx, :])
        sin_signed = jnp.where(first_half, -sin, sin)

        def rope(t):
            partner = jnp.where(first_half,
                                pltpu.roll(t, LANES - HEAD_DIM // 2, axis=1),
                                pltpu.roll(t, HEAD_DIM // 2, axis=1))
            return t * cos + partner * sin_signed

        def emit(ref, cols, val):
            if dil == 1:
                ref[:, cols] = val.astype(BF16)
            else:
                for r in range(dil):
                    ref[r, :, cols] = val[r * rows:(r + 1) * rows].astype(BF16)

        y = _rms_unit(h)
        a_kv = (y * gkv_ref[...]).astype(BF16)
        a_q = (y * gq_ref[...]).astype(BF16)
        q_ref, k_ref, v_ref = out_refs[3 * g:3 * g + 3]
        base = g * D_MODEL
        for c in range(0, D_MODEL, MXU_COLS):
            tq = _dot(a_q, wq_ref[:, base + c:base + c + MXU_COLS])
            tk = _dot(a_kv, wkv_ref[:, base + c:base + c + MXU_COLS])
            tv = _dot(a_kv, wkv_ref[:, Q_WIDTH + base + c:Q_WIDTH + base + c + MXU_COLS])
            for s in range(0, MXU_COLS, LANES):
                cols = slice(c + s, c + s + LANES)
                emit(q_ref, cols, rope(tq[:, s:s + LANES]) * scale)
                emit(k_ref, cols, rope(tk[:, s:s + LANES]))
            emit(v_ref, slice(c, c + MXU_COLS), tv)


def _qkv(h, pos, inv_freq, g_kv, g_q, w_kv, w_q):
    tokens, d = h.shape
    tm = TOKEN_TILE
    assert CHUNK % tm == 0 and tokens % CHUNK == 0
    tiles_per_chunk = CHUNK // tm
    n_chunks = tokens // CHUNK
    row = pl.BlockSpec((tm, d), lambda i: (i, 0))
    out_specs, out_shapes = [], []
    for dil in DILATIONS:
        if dil == 1:
            spec = row
            shape = jax.ShapeDtypeStruct((tokens, d), BF16)
        else:
            assert (tm // dil) % 16 == 0
            spec = pl.BlockSpec((None, dil, tm // dil, d),
                                lambda i: (i // tiles_per_chunk, 0, i % tiles_per_chunk, 0))
            shape = jax.ShapeDtypeStruct((n_chunks, dil, CHUNK // dil, d), BF16)
        out_specs += [spec] * 3
        out_shapes += [shape] * 3
    outs = pl.pallas_call(
        _qkv_kernel,
        grid=(tokens // tm,),
        in_specs=[row, pl.BlockSpec((tm, 1), lambda i: (i, 0)), _resident((1, LANES)),
                  _resident((1, d)), _resident((1, d)), _resident(w_kv.shape), _resident(w_q.shape)],
        out_specs=out_specs,
        out_shape=out_shapes,
        scratch_shapes=[pltpu.VMEM((N_SLABS, tm, LANES), F32), pltpu.VMEM((2, tm, LANES), F32)],
        compiler_params=_params("arbitrary"),
        name="qkv_rope",
    )(h, pos, inv_freq, g_kv, g_q, w_kv, w_q)
    return [o.reshape(n_chunks, CHUNK, d) for o in outs]


def _attn_kernel(*refs):
    n_in = 3 * N_BRANCHES
    qkv_refs, o_ref, scratch = refs[:n_in], refs[n_in], refs[n_in + 1:]
    kv_bufs, stat_refs = scratch[:2 * N_BRANCHES], scratch[2 * N_BRANCHES:]
    chunk = pl.program_id(2)

    lane = lax.broadcasted_iota(jnp.int32, (1, LANES), 1)
    low_head = lane < HEAD_DIM
    qi = lax.broadcasted_iota(jnp.int32, (BAND, 2 * BAND), 0)
    kj = lax.broadcasted_iota(jnp.int32, (BAND, 2 * BAND), 1)
    dist = qi + BAND - kj
    in_band = (dist >= 0) & (dist <= BAND)
    nt = (((1,), (1,)), ((), ()))

    def block(g, idx):
        dil = DILATIONS[g]
        run = CHUNK // dil
        per_run = run // BAND
        r = idx // per_run
        n = idx % per_run
        q_ref = qkv_refs[3 * g]
        kbuf_ref, vbuf_ref = kv_bufs[2 * g:2 * g + 2]
        q2 = q_ref[pl.ds(pl.multiple_of(idx * BAND, BAND), BAND), :]
        keys = pl.ds(pl.multiple_of(n * BAND, BAND), 2 * BAND)
        k2 = kbuf_ref[r, keys, :]
        v2 = vbuf_ref[r, keys, :]
        has_prev = jnp.logical_or(n > 0, chunk > 0)
        mask = in_band & ((kj >= BAND) | has_prev)
        stats = []
        for hh in range(HEADS_PER_VREG):
            mine = low_head if hh == 0 else jnp.logical_not(low_head)
            qh = jnp.where(mine, q2, jnp.zeros_like(q2))
            s = lax.dot_general(qh, k2, nt, preferred_element_type=F32)
            s = jnp.where(mask, s, NEG_INF)
            m = jnp.max(s, axis=-1, keepdims=True)
            p = jnp.exp(s - m)
            l = jnp.sum(p, axis=-1, keepdims=True)
            stats.append((m, l, _dot(p.astype(BF16), v2)))
        merged = [jnp.where(low_head, a, b) for a, b in zip(*stats)]
        return merged, n * BAND * dil + r

    for g, dil in enumerate(DILATIONS):
        run = CHUNK // dil
        q_ref, k_ref, v_ref = qkv_refs[3 * g:3 * g + 3]
        for src_ref, buf_ref in zip((k_ref, v_ref), kv_bufs[2 * g:2 * g + 2]):
            @pl.when(chunk == 0)
            def _(buf_ref=buf_ref):
                buf_ref[:, 0:BAND, :] = jnp.zeros((dil, BAND, LANES), BF16)

            @pl.when(chunk > 0)
            def _(buf_ref=buf_ref, run=run):
                buf_ref[:, 0:BAND, :] = buf_ref[:, run:run + BAND, :]

            buf_ref[:, BAND:BAND + run, :] = src_ref[...].reshape(dil, run, LANES)

    for g in range(N_BRANCHES - 1, 0, -1):
        dil = DILATIONS[g]
        m_ref, l_ref, acc_ref = stat_refs[3 * (g - 1):3 * g]

        def park(idx, carry, g=g, dil=dil, m_ref=m_ref, l_ref=l_ref, acc_ref=acc_ref):
            (m, l, acc), start = block(g, idx)
            rows = pl.ds(start, BAND, stride=dil)
            m_ref[rows, :] = jnp.broadcast_to(m, (BAND, LANES))
            l_ref[rows, :] = jnp.broadcast_to(l, (BAND, LANES))
            acc_ref[rows, :] = acc
            return carry

        lax.fori_loop(0, BLOCKS_PER_CHUNK, park, 0)

    def finish(idx, carry):
        (m0, l0, acc0), start = block(0, idx)
        rows = pl.ds(pl.multiple_of(start, BAND), BAND)
        ms = [jnp.broadcast_to(m0, (BAND, LANES))] + [stat_refs[3 * g][rows, :] for g in range(N_BRANCHES - 1)]
        ls = [jnp.broadcast_to(l0, (BAND, LANES))] + [stat_refs[3 * g + 1][rows, :] for g in range(N_BRANCHES - 1)]
        accs = [acc0] + [stat_refs[3 * g + 2][rows, :] for g in range(N_BRANCHES - 1)]
        top = ms[0]
        for m in ms[1:]:
            top = jnp.maximum(top, m)
        num = jnp.zeros((BAND, LANES), F32)
        den = jnp.zeros((BAND, LANES), F32)
        for m, l, acc in zip(ms, ls, accs):
            e = jnp.exp(m - top)
            num = num + e * acc
            den = den + e * l
        o_ref[rows, :] = (num / den).astype(o_ref.dtype)
        return carry

    lax.fori_loop(0, BLOCKS_PER_CHUNK, finish, 0)


def _attention(qkv, batch, seq):
    chunks_per_seq = seq // CHUNK
    blk = pl.BlockSpec((None, CHUNK, LANES), lambda b, j, c: (b * chunks_per_seq + c, 0, j))
    scratch = []
    for dil in DILATIONS:
        scratch += [pltpu.VMEM((dil, BAND + CHUNK // dil, LANES), BF16)] * 2
    scratch += [pltpu.VMEM((CHUNK, LANES), F32)] * (3 * (N_BRANCHES - 1))
    return pl.pallas_call(
        _attn_kernel,
        grid=(batch, N_HEAD_PAIRS, chunks_per_seq),
        in_specs=[blk] * (3 * N_BRANCHES),
        out_specs=pl.BlockSpec((CHUNK, LANES), lambda b, j, c: (b * chunks_per_seq + c, j)),
        out_shape=jax.ShapeDtypeStruct((batch * seq, D_MODEL), BF16),
        scratch_shapes=scratch,
        compiler_params=_params("arbitrary", "arbitrary", "arbitrary"),
        name="dilated_attn",
    )(*qkv)


def kernel(x, positions, mix_norm_pre, mix_norm_post, ffn_norm_pre, ffn_norm_post, ffn_w_gate_up,
           ffn_w_down, conv_w_in, conv_w, conv_w_out, kv_norm, w_kv, w_q, w_o):
    batch, seq, d = x.shape
    tokens = batch * seq
    for window, dilation in BRANCHES:
        assert window // dilation == BAND and seq % (BAND * dilation) == 0

    def gain(g):
        return g.reshape(1, d).astype(F32)

    half = HEAD_DIM // 2
    inv_freq = ROPE_THETA ** (-jnp.arange(half, dtype=F32) / half)
    inv_freq = jnp.tile(inv_freq, LANES // half).reshape(1, LANES)
    pos = positions.reshape(tokens, 1)

    h = _conv_mixer(x, gain(mix_norm_pre[0]), conv_w_in[0].astype(BF16), conv_w[0],
                    conv_w_out[0].astype(BF16), gain(mix_norm_post[0]))
    h = h.reshape(tokens, d)
    h = _ffn(h, gain(ffn_norm_pre[0]), ffn_w_gate_up[0].astype(BF16), ffn_w_down[0].astype(BF16),
             gain(ffn_norm_post[0]))

    qkv = _qkv(h, pos, inv_freq, gain(kv_norm), gain(mix_norm_pre[1]), w_kv.astype(BF16),
               w_q[0].astype(BF16))
    a = _attention(qkv, batch, seq)
    h = _ffn(h, gain(ffn_norm_pre[1]), ffn_w_gate_up[1].astype(BF16), ffn_w_down[1].astype(BF16),
             gain(ffn_norm_post[1]), attn=(a, w_o[0].astype(BF16), gain(mix_norm_post[1])))
    return h.reshape(batch, seq, d)
```

```python
import jax
import jax.numpy as jnp
from jax import lax
from jax.experimental import pallas as pl
from jax.experimental.pallas import tpu as pltpu

D_MODEL = 1024
HEAD_DIM = 64
N_HEADS = D_MODEL // HEAD_DIM
BRANCHES = ((128, 1), (512, 4), (2048, 16))
DILATIONS = tuple(d for _, d in BRANCHES)
N_BRANCHES = len(BRANCHES)
Q_WIDTH = N_BRANCHES * N_HEADS * HEAD_DIM
CONV_WIDTH = 3
ROPE_THETA = 10000.0
RMS_EPS = 1e-6
NEG_INF = -1e30

LANES = 128
MXU_COLS = 256
BAND = 128
CHUNK = BAND * max(DILATIONS)
BLOCKS_PER_CHUNK = CHUNK // BAND
HEADS_PER_VREG = LANES // HEAD_DIM
N_HEAD_PAIRS = N_HEADS // HEADS_PER_VREG
N_SLABS = D_MODEL // LANES
VMEM_LIMIT_BYTES = 56 * 1024 * 1024

TOKEN_TILE = 512
CARRY_ROWS = 8

F32 = jnp.float32
BF16 = jnp.bfloat16


def _dot(a, b):
    return jnp.dot(a, b, preferred_element_type=F32)


def _rms_unit(x):
    return x * lax.rsqrt(jnp.mean(x * x, axis=-1, keepdims=True) + RMS_EPS)


def _resident(shape):
    return pl.BlockSpec(shape, lambda *_: (0,) * len(shape), pipeline_mode=pl.Buffered(1))


def _params(*semantics):
    return pltpu.CompilerParams(dimension_semantics=semantics, vmem_limit_bytes=VMEM_LIMIT_BYTES)


def _conv_mixer_kernel(x_ref, gpre_ref, win_ref, cw_ref, wout_ref, gpost_ref, o_ref, ubuf_ref):
    j = pl.program_id(1)
    tm = x_ref.shape[0]
    x = x_ref[...]
    hn = (_rms_unit(x) * gpre_ref[...]).astype(BF16)
    b_gate = _dot(hn, win_ref[:, 0:D_MODEL])
    u = _dot(hn, win_ref[:, D_MODEL:2 * D_MODEL]) * _dot(hn, win_ref[:, 2 * D_MODEL:3 * D_MODEL])

    @pl.when(j == 0)
    def _():
        ubuf_ref[0:CARRY_ROWS, :] = jnp.zeros((CARRY_ROWS, D_MODEL), F32)

    @pl.when(j > 0)
    def _():
        ubuf_ref[0:CARRY_ROWS, :] = ubuf_ref[tm:tm + CARRY_ROWS, :]

    ubuf_ref[CARRY_ROWS:CARRY_ROWS + tm, :] = u
    u1 = ubuf_ref[CARRY_ROWS - 1:CARRY_ROWS - 1 + tm, :]
    u2 = ubuf_ref[CARRY_ROWS - 2:CARRY_ROWS - 2 + tm, :]
    conv = cw_ref[0:1, :] * u2 + cw_ref[1:2, :] * u1 + cw_ref[2:3, :] * u
    y = _dot((b_gate * conv).astype(BF16), wout_ref[...])
    o_ref[...] = x + _rms_unit(y) * gpost_ref[...]


def _conv_mixer(x, g_pre, w_in, conv_w, w_out, g_post):
    batch, seq, d = x.shape
    tm = TOKEN_TILE
    row = pl.BlockSpec((None, tm, d), lambda b, j: (b, j, 0))
    return pl.pallas_call(
        _conv_mixer_kernel,
        grid=(batch, seq // tm),
        in_specs=[row, _resident((1, d)), _resident(w_in.shape), _resident(conv_w.shape),
                  _resident(w_out.shape), _resident((1, d))],
        out_specs=row,
        out_shape=jax.ShapeDtypeStruct(x.shape, F32),
        scratch_shapes=[pltpu.VMEM((tm + CARRY_ROWS, d), F32)],
        compiler_params=_params("arbitrary", "arbitrary"),
        name="conv_mixer",
    )(x, g_pre, w_in, conv_w, w_out, g_post)


def _swiglu_residual(h, gpre_ref, wgu_ref, wd_ref, gpost_ref, act_ref):
    d_ff = wd_ref.shape[0]
    hn = (_rms_unit(h) * gpre_ref[...]).astype(BF16)
    for c in range(0, d_ff, MXU_COLS):
        g = _dot(hn, wgu_ref[:, c:c + MXU_COLS])
        u = _dot(hn, wgu_ref[:, d_ff + c:d_ff + c + MXU_COLS])
        act_ref[:, c:c + MXU_COLS] = (g * jax.nn.sigmoid(g) * u).astype(BF16)
    f = _dot(act_ref[...], wd_ref[...])
    return h + _rms_unit(f) * gpost_ref[...]


def _ffn_kernel(h_ref, gpre_ref, wgu_ref, wd_ref, gpost_ref, o_ref, act_ref):
    o_ref[...] = _swiglu_residual(h_ref[...], gpre_ref, wgu_ref, wd_ref, gpost_ref, act_ref)


def _attn_out_ffn_kernel(h_ref, a_ref, wo_ref, gmix_ref, gpre_ref, wgu_ref, wd_ref, gpost_ref,
                         o_ref, act_ref):
    h = h_ref[...] + _rms_unit(_dot(a_ref[...], wo_ref[...])) * gmix_ref[...]
    o_ref[...] = _swiglu_residual(h, gpre_ref, wgu_ref, wd_ref, gpost_ref, act_ref)


def _ffn(h, g_pre, w_gate_up, w_down, g_post, attn=None):
    tokens, d = h.shape
    d_ff = w_down.shape[0]
    assert d_ff % MXU_COLS == 0
    tm = TOKEN_TILE
    row = pl.BlockSpec((tm, d), lambda i: (i, 0))
    ffn_specs = [_resident((1, d)), _resident(w_gate_up.shape), _resident(w_down.shape),
                 _resident((1, d))]
    if attn is None:
        body, specs, args = _ffn_kernel, [row], (h,)
    else:
        a, w_o, g_mix = attn
        body = _attn_out_ffn_kernel
        specs = [row, row, _resident(w_o.shape), _resident((1, d))]
        args = (h, a, w_o, g_mix)
    return pl.pallas_call(
        body,
        grid=(tokens // tm,),
        in_specs=specs + ffn_specs,
        out_specs=row,
        out_shape=jax.ShapeDtypeStruct(h.shape, F32),
        scratch_shapes=[pltpu.VMEM((tm, d_ff), BF16)],
        compiler_params=_params("arbitrary"),
        name="ffn" if attn is None else "attn_out_ffn",
    )(*args, g_pre, w_gate_up, w_down, g_post)


def _qkv_kernel(h_ref, pos_ref, invf_ref, gkv_ref, gq_ref, wkv_ref, wq_ref, *rest):
    out_refs, (hbuf_ref, trig_ref) = rest[:3 * N_BRANCHES], rest[3 * N_BRANCHES:]
    tm = h_ref.shape[0]

    for s in range(N_SLABS):
        hbuf_ref[s] = h_ref[:, s * LANES:(s + 1) * LANES]
    ang = pos_ref[...].astype(F32) * invf_ref[...]
    trig_ref[0] = jnp.cos(ang)
    trig_ref[1] = jnp.sin(ang)

    lane = lax.broadcasted_iota(jnp.int32, (1, LANES), 1)
    first_half = (lane % HEAD_DIM) < (HEAD_DIM // 2)
    scale = HEAD_DIM ** -0.5

    for g, dil in enumerate(DILATIONS):
        rows = tm // dil

        def gather(read):
            if dil == 1:
                return read(slice(None))
            return jnp.concatenate([read(pl.ds(r, rows, stride=dil)) for r in range(dil)], axis=0)

        h = gather(lambda idx: jnp.concatenate([hbuf_ref[s, idx, :] for s in range(N_SLABS)], axis=1))
        cos = gather(lambda idx: trig_ref[0, idx, :])
        sin = gather(lambda idx: trig_ref[1, idx, :])
        sin_signed = jnp.where(first_half, -sin, sin)

        def rope(t):
            partner = jnp.where(first_half,
                                pltpu.roll(t, LANES - HEAD_DIM // 2, axis=1),
                                pltpu.roll(t, HEAD_DIM // 2, axis=1))
            return t * cos + partner * sin_signed

        def emit(ref, cols, val):
            if dil == 1:
                ref[:, cols] = val.astype(BF16)
            else:
                for r in range(dil):
                    ref[r, :, cols] = val[r * rows:(r + 1) * rows].astype(BF16)

        y = _rms_unit(h)
        a_kv = (y * gkv_ref[...]).astype(BF16)
        a_q = (y * gq_ref[...]).astype(BF16)
        q_ref, k_ref, v_ref = out_refs[3 * g:3 * g + 3]
        base = g * D_MODEL
        for c in range(0, D_MODEL, MXU_COLS):
            tq = _dot(a_q, wq_ref[:, base + c:base + c + MXU_COLS])
            tk = _dot(a_kv, wkv_ref[:, base + c:base + c + MXU_COLS])
            tv = _dot(a_kv, wkv_ref[:, Q_WIDTH + base + c:Q_WIDTH + base + c + MXU_COLS])
            for s in range(0, MXU_COLS, LANES):
                cols = slice(c + s, c + s + LANES)
                emit(q_ref, cols, rope(tq[:, s:s + LANES]) * scale)
                emit(k_ref, cols, rope(tk[:, s:s + LANES]))
            emit(v_ref, slice(c, c + MXU_COLS), tv)


def _qkv(h, pos, inv_freq, g_kv, g_q, w_kv, w_q):
    tokens, d = h.shape
    tm = TOKEN_TILE
    assert CHUNK % tm == 0 and tokens % CHUNK == 0
    tiles_per_chunk = CHUNK // tm
    n_chunks = tokens // CHUNK
    row = pl.BlockSpec((tm, d), lambda i: (i, 0))
    out_specs, out_shapes = [], []
    for dil in DILATIONS:
        if dil == 1:
            spec = row
            shape = jax.ShapeDtypeStruct((tokens, d), BF16)
        else:
            assert (tm // dil) % 16 == 0
            spec = pl.BlockSpec((None, dil, tm // dil, d),
                                lambda i: (i // tiles_per_chunk, 0, i % tiles_per_chunk, 0))
            shape = jax.ShapeDtypeStruct((n_chunks, dil, CHUNK // dil, d), BF16)
        out_specs += [spec] * 3
        out_shapes += [shape] * 3
    outs = pl.pallas_call(
        _qkv_kernel,
        grid=(tokens // tm,),
        in_specs=[row, pl.BlockSpec((tm, 1), lambda i: (i, 0)), _resident((1, LANES)),
                  _resident((1, d)), _resident((1, d)), _resident(w_kv.shape), _resident(w_q.shape)],
        out_specs=out_specs,
        out_shape=out_shapes,
        scratch_shapes=[pltpu.VMEM((N_SLABS, tm, LANES), F32), pltpu.VMEM((2, tm, LANES), F32)],
        compiler_params=_params("arbitrary"),
        name="qkv_rope",
    )(h, pos, inv_freq, g_kv, g_q, w_kv, w_q)
    return [o.reshape(n_chunks, CHUNK, d) for o in outs]


def _attn_kernel(*refs):
    n_in = 3 * N_BRANCHES
    qkv_refs, o_ref, scratch = refs[:n_in], refs[n_in], refs[n_in + 1:]
    kv_bufs, stat_refs = scratch[:2 * N_BRANCHES], scratch[2 * N_BRANCHES:]
    chunk = pl.program_id(2)

    lane = lax.broadcasted_iota(jnp.int32, (1, LANES), 1)
    low_head = lane < HEAD_DIM
    qi = lax.broadcasted_iota(jnp.int32, (BAND, 2 * BAND), 0)
    kj = lax.broadcasted_iota(jnp.int32, (BAND, 2 * BAND), 1)
    dist = qi + BAND - kj
    in_band = (dist >= 0) & (dist <= BAND)
    first_mask = in_band & ((kj >= BAND) | (chunk > 0))
    nt = (((1,), (1,)), ((), ()))

    def block(g, idx):
        dil = DILATIONS[g]
        run = CHUNK // dil
        per_run = run // BAND
        r = idx // per_run
        n = idx % per_run
        q_ref = qkv_refs[3 * g]
        kbuf_ref, vbuf_ref = kv_bufs[2 * g:2 * g + 2]
        q2 = q_ref[idx * BAND:(idx + 1) * BAND, :]
        k2 = kbuf_ref[r, n * BAND:(n + 2) * BAND, :]
        v2 = vbuf_ref[r, n * BAND:(n + 2) * BAND, :]
        mask = in_band if n > 0 else first_mask
        stats = []
        for hh in range(HEADS_PER_VREG):
            mine = low_head if hh == 0 else jnp.logical_not(low_head)
            qh = jnp.where(mine, q2, jnp.zeros_like(q2))
            s = lax.dot_general(qh, k2, nt, preferred_element_type=F32)
            s = jnp.where(mask, s, NEG_INF)
            m = jnp.max(s, axis=-1, keepdims=True)
            p = jnp.exp(s - m)
            l = jnp.sum(p, axis=-1, keepdims=True)
            stats.append((m, l, _dot(p.astype(BF16), v2)))
        merged = [jnp.where(low_head, a, b) for a, b in zip(*stats)]
        return merged, n * BAND * dil + r

    for g, dil in enumerate(DILATIONS):
        run = CHUNK // dil
        q_ref, k_ref, v_ref = qkv_refs[3 * g:3 * g + 3]
        for src_ref, buf_ref in zip((k_ref, v_ref), kv_bufs[2 * g:2 * g + 2]):
            @pl.when(chunk == 0)
            def _(buf_ref=buf_ref):
                buf_ref[:, 0:BAND, :] = jnp.zeros((dil, BAND, LANES), BF16)

            @pl.when(chunk > 0)
            def _(buf_ref=buf_ref, run=run):
                buf_ref[:, 0:BAND, :] = buf_ref[:, run:run + BAND, :]

            buf_ref[:, BAND:BAND + run, :] = src_ref[...].reshape(dil, run, LANES)

    for g in range(N_BRANCHES - 1, 0, -1):
        dil = DILATIONS[g]
        m_ref, l_ref, acc_ref = stat_refs[3 * (g - 1):3 * g]

        def park(idx, carry, g=g, dil=dil, m_ref=m_ref, l_ref=l_ref, acc_ref=acc_ref):
            (m, l, acc), start = block(g, idx)
            rows = pl.ds(start, BAND, stride=dil)
            m_ref[rows, :] = jnp.broadcast_to(m, (BAND, LANES))
            l_ref[rows, :] = jnp.broadcast_to(l, (BAND, LANES))
            acc_ref[rows, :] = acc
            return carry

        for idx in range(BLOCKS_PER_CHUNK):
            park(idx, 0)

    def finish(idx, carry):
        (m0, l0, acc0), start = block(0, idx)
        rows = slice(start, start + BAND)
        ms = [jnp.broadcast_to(m0, (BAND, LANES))] + [stat_refs[3 * g][rows, :] for g in range(N_BRANCHES - 1)]
        ls = [jnp.broadcast_to(l0, (BAND, LANES))] + [stat_refs[3 * g + 1][rows, :] for g in range(N_BRANCHES - 1)]
        accs = [acc0] + [stat_refs[3 * g + 2][rows, :] for g in range(N_BRANCHES - 1)]
        top = ms[0]
        for m in ms[1:]:
            top = jnp.maximum(top, m)
        num = jnp.zeros((BAND, LANES), F32)
        den = jnp.zeros((BAND, LANES), F32)
        for m, l, acc in zip(ms, ls, accs):
            e = jnp.exp(m - top)
            num = num + e * acc
            den = den + e * l
        o_ref[rows, :] = (num / den).astype(o_ref.dtype)
        return carry

    for idx in range(BLOCKS_PER_CHUNK):
        finish(idx, 0)


def _attention(qkv, batch, seq):
    chunks_per_seq = seq // CHUNK
    blk = pl.BlockSpec((None, CHUNK, LANES), lambda b, j, c: (b * chunks_per_seq + c, 0, j))
    scratch = []
    for dil in DILATIONS:
        scratch += [pltpu.VMEM((dil, BAND + CHUNK // dil, LANES), BF16)] * 2
    scratch += [pltpu.VMEM((CHUNK, LANES), F32)] * (3 * (N_BRANCHES - 1))
    return pl.pallas_call(
        _attn_kernel,
        grid=(batch, N_HEAD_PAIRS, chunks_per_seq),
        in_specs=[blk] * (3 * N_BRANCHES),
        out_specs=pl.BlockSpec((CHUNK, LANES), lambda b, j, c: (b * chunks_per_seq + c, j)),
        out_shape=jax.ShapeDtypeStruct((batch * seq, D_MODEL), BF16),
        scratch_shapes=scratch,
        compiler_params=_params("arbitrary", "arbitrary", "arbitrary"),
        name="dilated_attn",
    )(*qkv)


def kernel(x, positions, mix_norm_pre, mix_norm_post, ffn_norm_pre, ffn_norm_post, ffn_w_gate_up,
           ffn_w_down, conv_w_in, conv_w, conv_w_out, kv_norm, w_kv, w_q, w_o):
    batch, seq, d = x.shape
    tokens = batch * seq
    for window, dilation in BRANCHES:
        assert window // dilation == BAND and seq % (BAND * dilation) == 0

    def gain(g):
        return g.reshape(1, d).astype(F32)

    half = HEAD_DIM // 2
    inv_freq = ROPE_THETA ** (-jnp.arange(half, dtype=F32) / half)
    inv_freq = jnp.tile(inv_freq, LANES // half).reshape(1, LANES)
    pos = positions.reshape(tokens, 1)

    h = _conv_mixer(x, gain(mix_norm_pre[0]), conv_w_in[0].astype(BF16), conv_w[0],
                    conv_w_out[0].astype(BF16), gain(mix_norm_post[0]))
    h = h.reshape(tokens, d)
    h = _ffn(h, gain(ffn_norm_pre[0]), ffn_w_gate_up[0].astype(BF16), ffn_w_down[0].astype(BF16),
             gain(ffn_norm_post[0]))

    qkv = _qkv(h, pos, inv_freq, gain(kv_norm), gain(mix_norm_pre[1]), w_kv.astype(BF16),
               w_q[0].astype(BF16))
    a = _attention(qkv, batch, seq)
    h = _ffn(h, gain(ffn_norm_pre[1]), ffn_w_gate_up[1].astype(BF16), ffn_w_down[1].astype(BF16),
             gain(ffn_norm_post[1]), attn=(a, w_o[0].astype(BF16), gain(mix_norm_post[1])))
    return h.reshape(batch, seq, d)
```

```python
import math

import jax
import jax.numpy as jnp
from jax import lax
from jax.experimental import pallas as pl
from jax.experimental.pallas import tpu as pltpu

D_MODEL = 1024
HEAD_DIM = 64
N_HEADS = D_MODEL // HEAD_DIM
BRANCHES = ((128, 1), (512, 4), (2048, 16))
DILATIONS = tuple(d for _, d in BRANCHES)
N_BRANCHES = len(BRANCHES)
Q_WIDTH = N_BRANCHES * N_HEADS * HEAD_DIM
CONV_WIDTH = 3
ROPE_THETA = 10000.0
RMS_EPS = 1e-6
NEG_INF = -1e30
LOG2_E = math.log2(math.e)
F32_MAX = 3.0e38

LANES = 128
MXU_COLS = 256
BAND = 128
CHUNK = BAND * max(DILATIONS)
BLOCKS_PER_CHUNK = CHUNK // BAND
HEADS_PER_VREG = LANES // HEAD_DIM
N_HEAD_PAIRS = N_HEADS // HEADS_PER_VREG
N_SLABS = D_MODEL // LANES
VMEM_LIMIT_BYTES = 56 * 1024 * 1024

TOKEN_TILE = 512
CARRY_ROWS = 8

F32 = jnp.float32
BF16 = jnp.bfloat16


def _dot(a, b):
    return jnp.dot(a, b, preferred_element_type=F32)


def _rms_unit(x):
    return x * lax.rsqrt(jnp.mean(x * x, axis=-1, keepdims=True) + RMS_EPS)


def _resident(shape):
    return pl.BlockSpec(shape, lambda *_: (0,) * len(shape), pipeline_mode=pl.Buffered(1))


def _params(*semantics):
    return pltpu.CompilerParams(dimension_semantics=semantics, vmem_limit_bytes=VMEM_LIMIT_BYTES)


def _conv_mixer_kernel(x_ref, gpre_ref, win_ref, cw_ref, wout_ref, gpost_ref, o_ref, ubuf_ref):
    j = pl.program_id(1)
    tm = x_ref.shape[0]
    x = x_ref[...]
    hn = (_rms_unit(x) * gpre_ref[...]).astype(BF16)
    b_gate = _dot(hn, win_ref[:, 0:D_MODEL])
    u = _dot(hn, win_ref[:, D_MODEL:2 * D_MODEL]) * _dot(hn, win_ref[:, 2 * D_MODEL:3 * D_MODEL])

    @pl.when(j == 0)
    def _():
        ubuf_ref[0:CARRY_ROWS, :] = jnp.zeros((CARRY_ROWS, D_MODEL), F32)

    @pl.when(j > 0)
    def _():
        ubuf_ref[0:CARRY_ROWS, :] = ubuf_ref[tm:tm + CARRY_ROWS, :]

    ubuf_ref[CARRY_ROWS:CARRY_ROWS + tm, :] = u
    u1 = ubuf_ref[CARRY_ROWS - 1:CARRY_ROWS - 1 + tm, :]
    u2 = ubuf_ref[CARRY_ROWS - 2:CARRY_ROWS - 2 + tm, :]
    conv = cw_ref[0:1, :] * u2 + cw_ref[1:2, :] * u1 + cw_ref[2:3, :] * u
    y = _dot((b_gate * conv).astype(BF16), wout_ref[...])
    o_ref[...] = x + _rms_unit(y) * gpost_ref[...]


def _conv_mixer(x, g_pre, w_in, conv_w, w_out, g_post):
    batch, seq, d = x.shape
    tm = TOKEN_TILE
    row = pl.BlockSpec((None, tm, d), lambda b, j: (b, j, 0))
    return pl.pallas_call(
        _conv_mixer_kernel,
        grid=(batch, seq // tm),
        in_specs=[row, _resident((1, d)), _resident(w_in.shape), _resident(conv_w.shape),
                  _resident(w_out.shape), _resident((1, d))],
        out_specs=row,
        out_shape=jax.ShapeDtypeStruct(x.shape, F32),
        scratch_shapes=[pltpu.VMEM((tm + CARRY_ROWS, d), F32)],
        compiler_params=_params("arbitrary", "arbitrary"),
        name="conv_mixer",
    )(x, g_pre, w_in, conv_w, w_out, g_post)


def _swiglu_residual(h, gpre_ref, wgu_ref, wd_ref, gpost_ref, act_ref):
    d_ff = wd_ref.shape[0]
    hn = (_rms_unit(h) * gpre_ref[...]).astype(BF16)
    for c in range(0, d_ff, MXU_COLS):
        g = _dot(hn, wgu_ref[:, c:c + MXU_COLS])
        u = _dot(hn, wgu_ref[:, d_ff + c:d_ff + c + MXU_COLS])
        act_ref[:, c:c + MXU_COLS] = (g * jax.nn.sigmoid(g) * u).astype(BF16)
    f = _dot(act_ref[...], wd_ref[...])
    return h + _rms_unit(f) * gpost_ref[...]


def _ffn_kernel(h_ref, gpre_ref, wgu_ref, wd_ref, gpost_ref, o_ref, act_ref):
    o_ref[...] = _swiglu_residual(h_ref[...], gpre_ref, wgu_ref, wd_ref, gpost_ref, act_ref)


def _attn_out_ffn_kernel(h_ref, a_ref, wo_ref, gmix_ref, gpre_ref, wgu_ref, wd_ref, gpost_ref,
                         o_ref, act_ref):
    h = h_ref[...] + _rms_unit(_dot(a_ref[...], wo_ref[...])) * gmix_ref[...]
    o_ref[...] = _swiglu_residual(h, gpre_ref, wgu_ref, wd_ref, gpost_ref, act_ref)


def _ffn(h, g_pre, w_gate_up, w_down, g_post, attn=None):
    tokens, d = h.shape
    d_ff = w_down.shape[0]
    assert d_ff % MXU_COLS == 0
    tm = TOKEN_TILE
    row = pl.BlockSpec((tm, d), lambda i: (i, 0))
    ffn_specs = [_resident((1, d)), _resident(w_gate_up.shape), _resident(w_down.shape),
                 _resident((1, d))]
    if attn is None:
        body, specs, args = _ffn_kernel, [row], (h,)
    else:
        a, w_o, g_mix = attn
        body = _attn_out_ffn_kernel
        specs = [row, row, _resident(w_o.shape), _resident((1, d))]
        args = (h, a, w_o, g_mix)
    return pl.pallas_call(
        body,
        grid=(tokens // tm,),
        in_specs=specs + ffn_specs,
        out_specs=row,
        out_shape=jax.ShapeDtypeStruct(h.shape, F32),
        scratch_shapes=[pltpu.VMEM((tm, d_ff), BF16)],
        compiler_params=_params("arbitrary"),
        name="ffn" if attn is None else "attn_out_ffn",
    )(*args, g_pre, w_gate_up, w_down, g_post)


def _qkv_kernel(h_ref, pos_ref, invf_ref, gkv_ref, gq_ref, wkv_ref, wq_ref, *rest):
    out_refs, (hbuf_ref, trig_ref) = rest[:3 * N_BRANCHES], rest[3 * N_BRANCHES:]
    tm = h_ref.shape[0]

    for s in range(N_SLABS):
        hbuf_ref[s] = h_ref[:, s * LANES:(s + 1) * LANES]
    ang = pos_ref[...].astype(F32) * invf_ref[...]
    trig_ref[0] = jnp.cos(ang)
    trig_ref[1] = jnp.sin(ang)

    lane = lax.broadcasted_iota(jnp.int32, (1, LANES), 1)
    first_half = (lane % HEAD_DIM) < (HEAD_DIM // 2)
    scale = HEAD_DIM ** -0.5 * LOG2_E

    for g, dil in enumerate(DILATIONS):
        rows = tm // dil

        def gather(read):
            if dil == 1:
                return read(slice(None))
            return jnp.concatenate([read(pl.ds(r, rows, stride=dil)) for r in range(dil)], axis=0)

        h = gather(lambda idx: jnp.concatenate([hbuf_ref[s, idx, :] for s in range(N_SLABS)], axis=1))
        cos = gather(lambda idx: trig_ref[0, idx, :])
        sin = gather(lambda idx: trig_ref[1, idx, :])
        sin_signed = jnp.where(first_half, -sin, sin)

        def rope(t):
            partner = jnp.where(first_half,
                                pltpu.roll(t, LANES - HEAD_DIM // 2, axis=1),
                                pltpu.roll(t, HEAD_DIM // 2, axis=1))
            return t * cos + partner * sin_signed

        def emit(ref, cols, val):
            if dil == 1:
                ref[:, cols] = val.astype(BF16)
            else:
                for r in range(dil):
                    ref[r, :, cols] = val[r * rows:(r + 1) * rows].astype(BF16)

        y = _rms_unit(h)
        a_kv = (y * gkv_ref[...]).astype(BF16)
        a_q = (y * gq_ref[...]).astype(BF16)
        q_ref, k_ref, v_ref = out_refs[3 * g:3 * g + 3]
        base = g * D_MODEL
        for c in range(0, D_MODEL, MXU_COLS):
            tq = _dot(a_q, wq_ref[:, base + c:base + c + MXU_COLS])
            tk = _dot(a_kv, wkv_ref[:, base + c:base + c + MXU_COLS])
            tv = _dot(a_kv, wkv_ref[:, Q_WIDTH + base + c:Q_WIDTH + base + c + MXU_COLS])
            for s in range(0, MXU_COLS, LANES):
                cols = slice(c + s, c + s + LANES)
                emit(q_ref, cols, rope(tq[:, s:s + LANES]) * scale)
                emit(k_ref, cols, rope(tk[:, s:s + LANES]))
            emit(v_ref, slice(c, c + MXU_COLS), tv)


def _qkv(h, pos, inv_freq, g_kv, g_q, w_kv, w_q):
    tokens, d = h.shape
    tm = TOKEN_TILE
    assert CHUNK % tm == 0 and tokens % CHUNK == 0
    tiles_per_chunk = CHUNK // tm
    n_chunks = tokens // CHUNK
    row = pl.BlockSpec((tm, d), lambda i: (i, 0))
    out_specs, out_shapes = [], []
    for dil in DILATIONS:
        if dil == 1:
            spec = row
            shape = jax.ShapeDtypeStruct((tokens, d), BF16)
        else:
            assert (tm // dil) % 16 == 0
            spec = pl.BlockSpec((None, dil, tm // dil, d),
                                lambda i: (i // tiles_per_chunk, 0, i % tiles_per_chunk, 0))
            shape = jax.ShapeDtypeStruct((n_chunks, dil, CHUNK // dil, d), BF16)
        out_specs += [spec] * 3
        out_shapes += [shape] * 3
    outs = pl.pallas_call(
        _qkv_kernel,
        grid=(tokens // tm,),
        in_specs=[row, pl.BlockSpec((tm, 1), lambda i: (i, 0)), _resident((1, LANES)),
                  _resident((1, d)), _resident((1, d)), _resident(w_kv.shape), _resident(w_q.shape)],
        out_specs=out_specs,
        out_shape=out_shapes,
        scratch_shapes=[pltpu.VMEM((N_SLABS, tm, LANES), F32), pltpu.VMEM((2, tm, LANES), F32)],
        compiler_params=_params("arbitrary"),
        name="qkv_rope",
    )(h, pos, inv_freq, g_kv, g_q, w_kv, w_q)
    return [o.reshape(n_chunks, CHUNK, d) for o in outs]


def _attn_kernel(*refs):
    n_in = 3 * N_BRANCHES
    qkv_refs, o_ref, scratch = refs[:n_in], refs[n_in], refs[n_in + 1:]
    kv_bufs, stat_refs, cap_ref = scratch[:2 * N_BRANCHES], scratch[2 * N_BRANCHES:-1], scratch[-1]
    chunk = pl.program_id(2)

    lane = lax.broadcasted_iota(jnp.int32, (1, LANES), 1)
    low_head = lane < HEAD_DIM
    qi = lax.broadcasted_iota(jnp.int32, (BAND, 2 * BAND), 0)
    kj = lax.broadcasted_iota(jnp.int32, (BAND, 2 * BAND), 1)
    dist = qi + BAND - kj
    in_band = (dist >= 0) & (dist <= BAND)
    first_mask = in_band & ((kj >= BAND) | (chunk > 0))
    cap_ref[0] = jnp.where(in_band, F32_MAX, NEG_INF)
    cap_ref[1] = jnp.where(first_mask, F32_MAX, NEG_INF)
    ones = jnp.ones((2 * BAND, LANES), BF16)
    nt = (((1,), (1,)), ((), ()))

    def block(g, idx):
        dil = DILATIONS[g]
        run = CHUNK // dil
        per_run = run // BAND
        r = idx // per_run
        n = idx % per_run
        q_ref = qkv_refs[3 * g]
        kbuf_ref, vbuf_ref = kv_bufs[2 * g:2 * g + 2]
        q2 = q_ref[idx * BAND:(idx + 1) * BAND, :]
        k2 = kbuf_ref[r, n * BAND:(n + 2) * BAND, :]
        v2 = vbuf_ref[r, n * BAND:(n + 2) * BAND, :]
        zero = jnp.zeros_like(q2)
        qs = jnp.concatenate([jnp.where(low_head, q2, zero), jnp.where(low_head, zero, q2)], axis=0)
        s = lax.dot_general(qs, k2, nt, preferred_element_type=F32)
        cap = cap_ref[1 if n == 0 else 0]
        s = jnp.minimum(s, jnp.concatenate([cap, cap], axis=0))
        m = jnp.max(s, axis=-1, keepdims=True)
        p = jnp.exp2(s - m).astype(BF16)
        out = _dot(p, jnp.concatenate([v2, ones], axis=1))
        acc = jnp.where(low_head, out[0:BAND, 0:LANES], out[BAND:, 0:LANES])
        l = jnp.where(low_head, out[0:BAND, LANES:], out[BAND:, LANES:])
        m2 = jnp.where(low_head, m[0:BAND], m[BAND:])
        return (m2, l, acc), n * BAND * dil + r

    for g, dil in enumerate(DILATIONS):
        run = CHUNK // dil
        q_ref, k_ref, v_ref = qkv_refs[3 * g:3 * g + 3]
        for src_ref, buf_ref in zip((k_ref, v_ref), kv_bufs[2 * g:2 * g + 2]):
            @pl.when(chunk == 0)
            def _(buf_ref=buf_ref):
                buf_ref[:, 0:BAND, :] = jnp.zeros((dil, BAND, LANES), BF16)

            @pl.when(chunk > 0)
            def _(buf_ref=buf_ref, run=run):
                buf_ref[:, 0:BAND, :] = buf_ref[:, run:run + BAND, :]

            buf_ref[:, BAND:BAND + run, :] = src_ref[...].reshape(dil, run, LANES)

    for g in range(N_BRANCHES - 1, 0, -1):
        dil = DILATIONS[g]
        m_ref, l_ref, acc_ref = stat_refs[3 * (g - 1):3 * g]

        def park(idx, carry, g=g, dil=dil, m_ref=m_ref, l_ref=l_ref, acc_ref=acc_ref):
            (m, l, acc), start = block(g, idx)
            rows = pl.ds(start, BAND, stride=dil)
            m_ref[rows, :] = jnp.broadcast_to(m, (BAND, LANES))
            l_ref[rows, :] = jnp.broadcast_to(l, (BAND, LANES))
            acc_ref[rows, :] = acc
            return carry

        for idx in range(BLOCKS_PER_CHUNK):
            park(idx, 0)

    def finish(idx, carry):
        (m0, l0, acc0), start = block(0, idx)
        rows = slice(start, start + BAND)
        ms = [jnp.broadcast_to(m0, (BAND, LANES))] + [stat_refs[3 * g][rows, :] for g in range(N_BRANCHES - 1)]
        ls = [jnp.broadcast_to(l0, (BAND, LANES))] + [stat_refs[3 * g + 1][rows, :] for g in range(N_BRANCHES - 1)]
        accs = [acc0] + [stat_refs[3 * g + 2][rows, :] for g in range(N_BRANCHES - 1)]
        top = ms[0]
        for m in ms[1:]:
            top = jnp.maximum(top, m)
        num = jnp.zeros((BAND, LANES), F32)
        den = jnp.zeros((BAND, LANES), F32)
        for m, l, acc in zip(ms, ls, accs):
            e = jnp.exp2(m - top)
            num = num + e * acc
            den = den + e * l
        o_ref[rows, :] = (num / den).astype(o_ref.dtype)
        return carry

    for idx in range(BLOCKS_PER_CHUNK):
        finish(idx, 0)


def _attention(qkv, batch, seq):
    chunks_per_seq = seq // CHUNK
    blk = pl.BlockSpec((None, CHUNK, LANES), lambda b, j, c: (b * chunks_per_seq + c, 0, j))
    scratch = []
    for dil in DILATIONS:
        scratch += [pltpu.VMEM((dil, BAND + CHUNK // dil, LANES), BF16)] * 2
    scratch += [pltpu.VMEM((CHUNK, LANES), F32)] * (3 * (N_BRANCHES - 1))
    scratch += [pltpu.VMEM((2, BAND, 2 * BAND), F32)]
    return pl.pallas_call(
        _attn_kernel,
        grid=(batch, N_HEAD_PAIRS, chunks_per_seq),
        in_specs=[blk] * (3 * N_BRANCHES),
        out_specs=pl.BlockSpec((CHUNK, LANES), lambda b, j, c: (b * chunks_per_seq + c, j)),
        out_shape=jax.ShapeDtypeStruct((batch * seq, D_MODEL), BF16),
        scratch_shapes=scratch,
        compiler_params=_params("arbitrary", "arbitrary", "arbitrary"),
        name="dilated_attn",
    )(*qkv)


def kernel(x, positions, mix_norm_pre, mix_norm_post, ffn_norm_pre, ffn_norm_post, ffn_w_gate_up,
           ffn_w_down, conv_w_in, conv_w, conv_w_out, kv_norm, w_kv, w_q, w_o):
    batch, seq, d = x.shape
    tokens = batch * seq
    for window, dilation in BRANCHES:
        assert window // dilation == BAND and seq % (BAND * dilation) == 0

    def gain(g):
        return g.reshape(1, d).astype(F32)

    half = HEAD_DIM // 2
    inv_freq = ROPE_THETA ** (-jnp.arange(half, dtype=F32) / half)
    inv_freq = jnp.tile(inv_freq, LANES // half).reshape(1, LANES)
    pos = positions.reshape(tokens, 1)

    h = _conv_mixer(x, gain(mix_norm_pre[0]), conv_w_in[0].astype(BF16), conv_w[0],
                    conv_w_out[0].astype(BF16), gain(mix_norm_post[0]))
    h = h.reshape(tokens, d)
    h = _ffn(h, gain(ffn_norm_pre[0]), ffn_w_gate_up[0].astype(BF16), ffn_w_down[0].astype(BF16),
             gain(ffn_norm_post[0]))

    qkv = _qkv(h, pos, inv_freq, gain(kv_norm), gain(mix_norm_pre[1]), w_kv.astype(BF16),
               w_q[0].astype(BF16))
    a = _attention(qkv, batch, seq)
    h = _ffn(h, gain(ffn_norm_pre[1]), ffn_w_gate_up[1].astype(BF16), ffn_w_down[1].astype(BF16),
             gain(ffn_norm_post[1]), attn=(a, w_o[0].astype(BF16), gain(mix_norm_post[1])))
    return h.reshape(batch, seq, d)
```

```python
import math

import jax
import jax.numpy as jnp
from jax import lax
from jax.experimental import pallas as pl
from jax.experimental.pallas import tpu as pltpu

D_MODEL = 1024
HEAD_DIM = 64
N_HEADS = D_MODEL // HEAD_DIM
BRANCHES = ((128, 1), (512, 4), (2048, 16))
DILATIONS = tuple(d for _, d in BRANCHES)
N_BRANCHES = len(BRANCHES)
Q_WIDTH = N_BRANCHES * N_HEADS * HEAD_DIM
CONV_WIDTH = 3
ROPE_THETA = 10000.0
RMS_EPS = 1e-6
NEG_INF = -1e30
LOG2_E = math.log2(math.e)
F32_MAX = 3.0e38

LANES = 128
BF16_SUBLANES = 16
MXU_COLS = 256
BAND = 128
CHUNK = BAND * max(DILATIONS)
BLOCKS_PER_CHUNK = CHUNK // BAND
HEADS_PER_VREG = LANES // HEAD_DIM
N_HEAD_PAIRS = N_HEADS // HEADS_PER_VREG
N_SLABS = D_MODEL // LANES
VMEM_LIMIT_BYTES = 56 * 1024 * 1024

TOKEN_TILE = 512
ROW_SPANS = 2
CARRY_ROWS = 8

F32 = jnp.float32
BF16 = jnp.bfloat16


def _dot(a, b):
    return jnp.dot(a, b, preferred_element_type=F32)


def _rms_unit(x):
    return x * lax.rsqrt(jnp.mean(x * x, axis=-1, keepdims=True) + RMS_EPS)


def _resident(shape, index=None):
    index = (0,) * len(shape) if index is None else index
    return pl.BlockSpec(shape, lambda *_: index, pipeline_mode=pl.Buffered(1))


def _params(*semantics):
    return pltpu.CompilerParams(dimension_semantics=semantics, vmem_limit_bytes=VMEM_LIMIT_BYTES)


def _cast_plan(weights, steps, linear_step):
    in_specs, out_specs, out_shapes = [], [], []
    for w, layer in weights:
        _, rows, cols = w.shape
        block = -(-rows // steps)
        block = -(-block // BF16_SUBLANES) * BF16_SUBLANES
        while rows % block:
            block += BF16_SUBLANES
        last = rows // block - 1

        def src(*idx, layer=layer, last=last):
            return (layer, jnp.minimum(linear_step(*idx), last), 0)

        def dst(*idx, last=last):
            return (jnp.minimum(linear_step(*idx), last), 0)

        in_specs.append(pl.BlockSpec((None, block, cols), src))
        out_specs.append(pl.BlockSpec((block, cols), dst))
        out_shapes.append(jax.ShapeDtypeStruct((rows, cols), BF16))
    return in_specs, out_specs, out_shapes


def _with_casts(body, n_in, n_out, n_cast):
    def kernel(*refs):
        ins, refs = refs[:n_in], refs[n_in:]
        cast_src, refs = refs[:n_cast], refs[n_cast:]
        outs, refs = refs[:n_out], refs[n_out:]
        cast_dst, scratch = refs[:n_cast], refs[n_cast:]
        body(*ins, *outs, *scratch)
        for s_ref, d_ref in zip(cast_src, cast_dst):
            d_ref[...] = s_ref[...].astype(BF16)
    return kernel


def _conv_mixer_kernel(x_ref, gpre_ref, win_ref, cw_ref, wout_ref, gpost_ref, o_ref, ubuf_ref):
    j = pl.program_id(1)
    tm = x_ref.shape[0]

    @pl.when(j == 0)
    def _():
        ubuf_ref[0:CARRY_ROWS, :] = jnp.zeros((CARRY_ROWS, D_MODEL), F32)

    @pl.when(j > 0)
    def _():
        ubuf_ref[0:CARRY_ROWS, :] = ubuf_ref[tm:tm + CARRY_ROWS, :]

    span = tm // ROW_SPANS
    xs = [x_ref[k * span:(k + 1) * span, :] for k in range(ROW_SPANS)]
    hns = [(_rms_unit(x) * gpre_ref[...]).astype(BF16) for x in xs]
    gates = [_dot(hn, win_ref[:, 0:D_MODEL]) for hn in hns]
    us = [_dot(hn, win_ref[:, D_MODEL:2 * D_MODEL]) * _dot(hn, win_ref[:, 2 * D_MODEL:3 * D_MODEL])
          for hn in hns]
    ys = []
    for k, (gate, u) in enumerate(zip(gates, us)):
        base = CARRY_ROWS + k * span
        ubuf_ref[base:base + span, :] = u
        u1 = ubuf_ref[base - 1:base - 1 + span, :]
        u2 = ubuf_ref[base - 2:base - 2 + span, :]
        conv = cw_ref[0:1, :] * u2 + cw_ref[1:2, :] * u1 + cw_ref[2:3, :] * u
        ys.append(_dot((gate * conv).astype(BF16), wout_ref[...]))
    for k, (x, y) in enumerate(zip(xs, ys)):
        o_ref[k * span:(k + 1) * span, :] = x + _rms_unit(y) * gpost_ref[...]


def _conv_mixer(x, g_pre, w_in, conv_w, w_out, g_post, casts):
    batch, seq, d = x.shape
    tm = TOKEN_TILE
    tiles = seq // tm
    row = pl.BlockSpec((None, tm, d), lambda b, j: (b, j, 0))
    c_in, c_out, c_shape = _cast_plan(casts, batch * tiles, lambda b, j: b * tiles + j)
    in_specs = [row, _resident((1, d)), _resident(w_in.shape), _resident(conv_w.shape),
                _resident(w_out.shape), _resident((1, d))]
    outs = pl.pallas_call(
        _with_casts(_conv_mixer_kernel, len(in_specs), 1, len(casts)),
        grid=(batch, tiles),
        in_specs=in_specs + c_in,
        out_specs=[row] + c_out,
        out_shape=[jax.ShapeDtypeStruct(x.shape, F32)] + c_shape,
        scratch_shapes=[pltpu.VMEM((tm + CARRY_ROWS, d), F32)],
        compiler_params=_params("arbitrary", "arbitrary"),
        name="conv_mixer",
    )(x, g_pre, w_in, conv_w, w_out, g_post, *[w for w, _ in casts])
    return outs[0], outs[1:]


def _swiglu_residual(hs, gpre_ref, wgu_ref, wd_ref, gpost_ref, act_ref):
    d_ff = wd_ref.shape[0]
    span = hs[0].shape[0]
    hns = [(_rms_unit(h) * gpre_ref[...]).astype(BF16) for h in hs]
    for c in range(0, d_ff, MXU_COLS):
        for k, hn in enumerate(hns):
            g = _dot(hn, wgu_ref[:, c:c + MXU_COLS])
            u = _dot(hn, wgu_ref[:, d_ff + c:d_ff + c + MXU_COLS])
            act_ref[k * span:(k + 1) * span, c:c + MXU_COLS] = (g * jax.nn.sigmoid(g) * u).astype(BF16)
    fs = [_dot(act_ref[k * span:(k + 1) * span, :], wd_ref[...]) for k in range(len(hs))]
    return [h + _rms_unit(f) * gpost_ref[...] for h, f in zip(hs, fs)]


def _row_spans(ref):
    span = ref.shape[0] // ROW_SPANS
    return [ref[k * span:(k + 1) * span, :] for k in range(ROW_SPANS)]


def _ffn_kernel(h_ref, gpre_ref, wgu_ref, wd_ref, gpost_ref, o_ref, act_ref):
    outs = _swiglu_residual(_row_spans(h_ref), gpre_ref, wgu_ref, wd_ref, gpost_ref, act_ref)
    o_ref[...] = jnp.concatenate(outs, axis=0)


def _attn_out_ffn_kernel(h_ref, a_ref, wo_ref, gmix_ref, gpre_ref, wgu_ref, wd_ref, gpost_ref,
                         o_ref, act_ref):
    hs = [h + _rms_unit(_dot(a, wo_ref[...])) * gmix_ref[...]
          for h, a in zip(_row_spans(h_ref), _row_spans(a_ref))]
    outs = _swiglu_residual(hs, gpre_ref, wgu_ref, wd_ref, gpost_ref, act_ref)
    o_ref[...] = jnp.concatenate(outs, axis=0)


def _ffn(h, g_pre, w_gate_up, w_down, g_post, attn=None, casts=()):
    tokens, d = h.shape
    d_ff = w_down.shape[0]
    assert d_ff % MXU_COLS == 0
    tm = TOKEN_TILE
    row = pl.BlockSpec((tm, d), lambda i: (i, 0))
    ffn_specs = [_resident((1, d)), _resident(w_gate_up.shape), _resident(w_down.shape),
                 _resident((1, d))]
    if attn is None:
        body, specs, args = _ffn_kernel, [row], (h,)
    else:
        a, w_o, g_mix = attn
        body = _attn_out_ffn_kernel
        specs = [row, row, _resident(w_o.shape), _resident((1, d))]
        args = (h, a, w_o, g_mix)
    in_specs = specs + ffn_specs
    c_in, c_out, c_shape = _cast_plan(casts, tokens // tm, lambda i: i)
    outs = pl.pallas_call(
        _with_casts(body, len(in_specs), 1, len(casts)),
        grid=(tokens // tm,),
        in_specs=in_specs + c_in,
        out_specs=[row] + c_out,
        out_shape=[jax.ShapeDtypeStruct(h.shape, F32)] + c_shape,
        scratch_shapes=[pltpu.VMEM((tm, d_ff), BF16)],
        compiler_params=_params("arbitrary"),
        name="ffn" if attn is None else "attn_out_ffn",
    )(*args, g_pre, w_gate_up, w_down, g_post, *[w for w, _ in casts])
    return outs[0], outs[1:]


def _qkv_kernel(h_ref, pos_ref, invf_ref, gkv_ref, gq_ref, wkv_ref, wq_ref, *rest):
    out_refs, (hbuf_ref, trig_ref) = rest[:3 * N_BRANCHES], rest[3 * N_BRANCHES:]
    tm = h_ref.shape[0]

    for s in range(N_SLABS):
        hbuf_ref[s] = h_ref[:, s * LANES:(s + 1) * LANES]

    lane = lax.broadcasted_iota(jnp.int32, (1, LANES), 1)
    half = HEAD_DIM // 2
    groups = LANES // half
    packed = tm // groups
    group = lane // half
    pos = pos_ref[...].astype(F32)
    pos_packed = jnp.broadcast_to(pos[0:packed], (packed, LANES))
    for j in range(1, groups):
        pos_packed = jnp.where(group == j, pos[j * packed:(j + 1) * packed], pos_packed)
    ang = pos_packed * invf_ref[...]
    for t, table in enumerate((jnp.cos(ang), jnp.sin(ang))):
        for j in range(groups):
            x = jnp.where(group == j, table, 0.0)
            shift = LANES // 2
            while shift >= half:
                x = x + pltpu.roll(x, shift, axis=1)
                shift //= 2
            trig_ref[t, j * packed:(j + 1) * packed, :] = x

    first_half = (lane % HEAD_DIM) < (HEAD_DIM // 2)
    scale = HEAD_DIM ** -0.5 * LOG2_E

    for g, dil in enumerate(DILATIONS):
        rows = tm // dil

        def gather(read):
            if dil == 1:
                return read(slice(None))
            return jnp.concatenate([read(pl.ds(r, rows, stride=dil)) for r in range(dil)], axis=0)

        h = gather(lambda idx: jnp.concatenate([hbuf_ref[s, idx, :] for s in range(N_SLABS)], axis=1))
        cos = gather(lambda idx: trig_ref[0, idx, :])
        sin = gather(lambda idx: trig_ref[1, idx, :])
        sin_signed = jnp.where(first_half, -sin, sin)

        def rope(t):
            partner = jnp.where(first_half,
                                pltpu.roll(t, LANES - HEAD_DIM // 2, axis=1),
                                pltpu.roll(t, HEAD_DIM // 2, axis=1))
            return t * cos + partner * sin_signed

        def emit(ref, cols, val):
            if dil == 1:
                ref[:, cols] = val.astype(BF16)
            else:
                for r in range(dil):
                    ref[r, :, cols] = val[r * rows:(r + 1) * rows].astype(BF16)

        y = _rms_unit(h)
        a_kv = (y * gkv_ref[...]).astype(BF16)
        a_q = (y * gq_ref[...]).astype(BF16)
        q_ref, k_ref, v_ref = out_refs[3 * g:3 * g + 3]
        base = g * D_MODEL
        for c in range(0, D_MODEL, MXU_COLS):
            tq = _dot(a_q, wq_ref[:, base + c:base + c + MXU_COLS])
            tk = _dot(a_kv, wkv_ref[:, base + c:base + c + MXU_COLS])
            tv = _dot(a_kv, wkv_ref[:, Q_WIDTH + base + c:Q_WIDTH + base + c + MXU_COLS])
            for s in range(0, MXU_COLS, LANES):
                cols = slice(c + s, c + s + LANES)
                emit(q_ref, cols, rope(tq[:, s:s + LANES]) * scale)
                emit(k_ref, cols, rope(tk[:, s:s + LANES]))
            emit(v_ref, slice(c, c + MXU_COLS), tv)


def _qkv(h, pos, inv_freq, g_kv, g_q, w_kv, w_q):
    tokens, d = h.shape
    tm = TOKEN_TILE
    assert CHUNK % tm == 0 and tokens % CHUNK == 0
    tiles_per_chunk = CHUNK // tm
    n_chunks = tokens // CHUNK
    row = pl.BlockSpec((tm, d), lambda i: (i, 0))
    out_specs, out_shapes = [], []
    for dil in DILATIONS:
        if dil == 1:
            spec = row
            shape = jax.ShapeDtypeStruct((tokens, d), BF16)
        else:
            assert (tm // dil) % BF16_SUBLANES == 0
            spec = pl.BlockSpec((None, dil, tm // dil, d),
                                lambda i: (i // tiles_per_chunk, 0, i % tiles_per_chunk, 0))
            shape = jax.ShapeDtypeStruct((n_chunks, dil, CHUNK // dil, d), BF16)
        out_specs += [spec] * 3
        out_shapes += [shape] * 3
    outs = pl.pallas_call(
        _qkv_kernel,
        grid=(tokens // tm,),
        in_specs=[row, pl.BlockSpec((tm, 1), lambda i: (i, 0)), _resident((1, LANES)),
                  _resident((1, d)), _resident((1, d)), _resident(w_kv.shape), _resident(w_q.shape)],
        out_specs=out_specs,
        out_shape=out_shapes,
        scratch_shapes=[pltpu.VMEM((N_SLABS, tm, LANES), F32), pltpu.VMEM((2, tm, LANES), F32)],
        compiler_params=_params("arbitrary"),
        name="qkv_rope",
    )(h, pos, inv_freq, g_kv, g_q, w_kv, w_q)
    return [o.reshape(n_chunks, CHUNK, d) for o in outs]


def _attn_kernel(*refs):
    n_in = 3 * N_BRANCHES
    qkv_refs, o_ref, scratch = refs[:n_in], refs[n_in], refs[n_in + 1:]
    kv_bufs, stat_refs, cap_ref = scratch[:2 * N_BRANCHES], scratch[2 * N_BRANCHES:-1], scratch[-1]
    chunk = pl.program_id(2)

    lane = lax.broadcasted_iota(jnp.int32, (1, LANES), 1)
    low_head = lane < HEAD_DIM
    qi = lax.broadcasted_iota(jnp.int32, (BAND, 2 * BAND), 0)
    kj = lax.broadcasted_iota(jnp.int32, (BAND, 2 * BAND), 1)
    dist = qi + BAND - kj
    in_band = (dist >= 0) & (dist <= BAND)
    first_mask = in_band & ((kj >= BAND) | (chunk > 0))
    cap_ref[0] = jnp.where(in_band, F32_MAX, NEG_INF)
    cap_ref[1] = jnp.where(first_mask, F32_MAX, NEG_INF)
    ones = jnp.ones((2 * BAND, LANES), BF16)
    nt = (((1,), (1,)), ((), ()))

    def block(g, idx):
        dil = DILATIONS[g]
        run = CHUNK // dil
        per_run = run // BAND
        r = idx // per_run
        n = idx % per_run
        q_ref = qkv_refs[3 * g]
        kbuf_ref, vbuf_ref = kv_bufs[2 * g:2 * g + 2]
        q2 = q_ref[idx * BAND:(idx + 1) * BAND, :]
        k2 = kbuf_ref[r, n * BAND:(n + 2) * BAND, :]
        v2 = vbuf_ref[r, n * BAND:(n + 2) * BAND, :]
        zero = jnp.zeros_like(q2)
        qs = jnp.concatenate([jnp.where(low_head, q2, zero), jnp.where(low_head, zero, q2)], axis=0)
        s = lax.dot_general(qs, k2, nt, preferred_element_type=F32)
        cap = cap_ref[1 if n == 0 else 0]
        s = jnp.minimum(s, jnp.concatenate([cap, cap], axis=0))
        m = jnp.max(s, axis=-1, keepdims=True)
        p = jnp.exp2(s - m).astype(BF16)
        out = _dot(p, jnp.concatenate([v2, ones], axis=1))
        acc = jnp.where(low_head, out[0:BAND, 0:LANES], out[BAND:, 0:LANES])
        l = jnp.where(low_head, out[0:BAND, LANES:], out[BAND:, LANES:])
        m2 = jnp.where(low_head, m[0:BAND], m[BAND:])
        return (m2, l, acc), n * BAND * dil + r

    for g, dil in enumerate(DILATIONS):
        run = CHUNK // dil
        q_ref, k_ref, v_ref = qkv_refs[3 * g:3 * g + 3]
        for src_ref, buf_ref in zip((k_ref, v_ref), kv_bufs[2 * g:2 * g + 2]):
            @pl.when(chunk == 0)
            def _(buf_ref=buf_ref):
                buf_ref[:, 0:BAND, :] = jnp.zeros((dil, BAND, LANES), BF16)

            @pl.when(chunk > 0)
            def _(buf_ref=buf_ref, run=run):
                buf_ref[:, 0:BAND, :] = buf_ref[:, run:run + BAND, :]

            buf_ref[:, BAND:BAND + run, :] = src_ref[...].reshape(dil, run, LANES)

    for g in range(N_BRANCHES - 1, 0, -1):
        dil = DILATIONS[g]
        m_ref, l_ref, acc_ref = stat_refs[3 * (g - 1):3 * g]
        for idx in range(BLOCKS_PER_CHUNK):
            (m, l, acc), start = block(g, idx)
            rows = pl.ds(start, BAND, stride=dil)
            m_ref[rows, :] = m
            l_ref[rows, :] = l
            acc_ref[rows, :] = acc

    for idx in range(BLOCKS_PER_CHUNK):
        (m0, l0, acc0), start = block(0, idx)
        rows = slice(start, start + BAND)
        ms = [m0] + [stat_refs[3 * g][rows, :] for g in range(N_BRANCHES - 1)]
        ls = [l0] + [stat_refs[3 * g + 1][rows, :] for g in range(N_BRANCHES - 1)]
        accs = [acc0] + [stat_refs[3 * g + 2][rows, :] for g in range(N_BRANCHES - 1)]
        top = ms[0]
        for m in ms[1:]:
            top = jnp.maximum(top, m)
        num = jnp.zeros((BAND, LANES), F32)
        den = jnp.zeros((BAND, LANES), F32)
        for m, l, acc in zip(ms, ls, accs):
            e = jnp.exp2(m - top)
            num = num + e * acc
            den = den + e * l
        o_ref[rows, :] = (num / den).astype(o_ref.dtype)


def _attention(qkv, batch, seq, casts):
    chunks_per_seq = seq // CHUNK
    blk = pl.BlockSpec((None, CHUNK, LANES), lambda b, j, c: (b * chunks_per_seq + c, 0, j))
    scratch = []
    for dil in DILATIONS:
        scratch += [pltpu.VMEM((dil, BAND + CHUNK // dil, LANES), BF16)] * 2
    scratch += [pltpu.VMEM((CHUNK, LANES), F32)] * (3 * (N_BRANCHES - 1))
    scratch += [pltpu.VMEM((2, BAND, 2 * BAND), F32)]
    steps = batch * N_HEAD_PAIRS * chunks_per_seq
    c_in, c_out, c_shape = _cast_plan(
        casts, steps, lambda b, j, c: (b * N_HEAD_PAIRS + j) * chunks_per_seq + c)
    n_in = 3 * N_BRANCHES
    outs = pl.pallas_call(
        _with_casts(_attn_kernel, n_in, 1, len(casts)),
        grid=(batch, N_HEAD_PAIRS, chunks_per_seq),
        in_specs=[blk] * n_in + c_in,
        out_specs=[pl.BlockSpec((CHUNK, LANES), lambda b, j, c: (b * chunks_per_seq + c, j))] + c_out,
        out_shape=[jax.ShapeDtypeStruct((batch * seq, D_MODEL), BF16)] + c_shape,
        scratch_shapes=scratch,
        compiler_params=_params("arbitrary", "arbitrary", "arbitrary"),
        name="dilated_attn",
    )(*qkv, *[w for w, _ in casts])
    return outs[0], outs[1:]


def kernel(x, positions, mix_norm_pre, mix_norm_post, ffn_norm_pre, ffn_norm_post, ffn_w_gate_up,
           ffn_w_down, conv_w_in, conv_w, conv_w_out, kv_norm, w_kv, w_q, w_o):
    batch, seq, d = x.shape
    tokens = batch * seq
    for window, dilation in BRANCHES:
        assert window // dilation == BAND and seq % (BAND * dilation) == 0

    def gain(g):
        return g.reshape(1, d).astype(F32)

    half = HEAD_DIM // 2
    inv_freq = ROPE_THETA ** (-jnp.arange(half, dtype=F32) / half)
    inv_freq = jnp.tile(inv_freq, LANES // half).reshape(1, LANES)
    pos = positions.reshape(tokens, 1)

    h, (w_gu0, w_dn0) = _conv_mixer(
        x, gain(mix_norm_pre[0]), conv_w_in[0].astype(BF16), conv_w[0], conv_w_out[0].astype(BF16),
        gain(mix_norm_post[0]), casts=[(ffn_w_gate_up, 0), (ffn_w_down, 0)])
    h = h.reshape(tokens, d)
    h, (w_kv16, w_q16) = _ffn(h, gain(ffn_norm_pre[0]), w_gu0, w_dn0, gain(ffn_norm_post[0]),
                              casts=[(w_kv[None], 0), (w_q, 0)])

    qkv = _qkv(h, pos, inv_freq, gain(kv_norm), gain(mix_norm_pre[1]), w_kv16, w_q16)
    a, (w_gu1, w_dn1, w_o16) = _attention(
        qkv, batch, seq, casts=[(ffn_w_gate_up, 1), (ffn_w_down, 1), (w_o, 0)])
    h, _ = _ffn(h, gain(ffn_norm_pre[1]), w_gu1, w_dn1, gain(ffn_norm_post[1]),
                attn=(a, w_o16, gain(mix_norm_post[1])))
    return h.reshape(batch, seq, d)
```

```python
import math

import jax
import jax.numpy as jnp
from jax import lax
from jax.experimental import pallas as pl
from jax.experimental.pallas import tpu as pltpu

D_MODEL = 1024
HEAD_DIM = 64
N_HEADS = D_MODEL // HEAD_DIM
BRANCHES = ((128, 1), (512, 4), (2048, 16))
DILATIONS = tuple(d for _, d in BRANCHES)
N_BRANCHES = len(BRANCHES)
Q_WIDTH = N_BRANCHES * N_HEADS * HEAD_DIM
CONV_WIDTH = 3
ROPE_THETA = 10000.0
RMS_EPS = 1e-6
NEG_INF = -1e30
LOG2_E = math.log2(math.e)
F32_MAX = 3.0e38

LANES = 128
BF16_SUBLANES = 16
MXU_COLS = 256
BAND = 128
CHUNK = BAND * max(DILATIONS)
BLOCKS_PER_CHUNK = CHUNK // BAND
HEADS_PER_VREG = LANES // HEAD_DIM
N_HEAD_PAIRS = N_HEADS // HEADS_PER_VREG
N_SLABS = D_MODEL // LANES
VMEM_LIMIT_BYTES = 56 * 1024 * 1024

TOKEN_TILE = 512
ROW_SPANS = 2
CARRY_ROWS = 8

F32 = jnp.float32
BF16 = jnp.bfloat16


def _dot(a, b):
    return jnp.dot(a, b, preferred_element_type=F32)


def _rms_unit(x):
    return x * lax.rsqrt(jnp.mean(x * x, axis=-1, keepdims=True) + RMS_EPS)


def _resident(shape):
    return pl.BlockSpec(shape, lambda *_: (0,) * len(shape), pipeline_mode=pl.Buffered(1))


def _params(*semantics):
    return pltpu.CompilerParams(dimension_semantics=semantics, vmem_limit_bytes=VMEM_LIMIT_BYTES)


def _cast_plan(weights, steps, linear_step):
    in_specs, out_specs, out_shapes = [], [], []
    for w, layer in weights:
        _, rows, cols = w.shape
        block = -(-rows // steps)
        block = -(-block // BF16_SUBLANES) * BF16_SUBLANES
        while rows % block:
            block += BF16_SUBLANES
        last = rows // block - 1

        def src(*idx, layer=layer, last=last):
            return (layer, jnp.minimum(linear_step(*idx), last), 0)

        def dst(*idx, last=last):
            return (jnp.minimum(linear_step(*idx), last), 0)

        in_specs.append(pl.BlockSpec((None, block, cols), src))
        out_specs.append(pl.BlockSpec((block, cols), dst))
        out_shapes.append(jax.ShapeDtypeStruct((rows, cols), BF16))
    return in_specs, out_specs, out_shapes


def _with_casts(body, n_in, n_out, n_cast):
    def kernel(*refs):
        ins, refs = refs[:n_in], refs[n_in:]
        cast_src, refs = refs[:n_cast], refs[n_cast:]
        outs, refs = refs[:n_out], refs[n_out:]
        cast_dst, scratch = refs[:n_cast], refs[n_cast:]
        body(*ins, *outs, *scratch)
        for s_ref, d_ref in zip(cast_src, cast_dst):
            d_ref[...] = s_ref[...].astype(BF16)
    return kernel


def _conv_mixer_kernel(x_ref, gpre_ref, win_ref, cw_ref, wout_ref, gpost_ref, o_ref, ubuf_ref):
    j = pl.program_id(1)
    tm = x_ref.shape[0]

    @pl.when(j == 0)
    def _():
        ubuf_ref[0:CARRY_ROWS, :] = jnp.zeros((CARRY_ROWS, D_MODEL), F32)

    @pl.when(j > 0)
    def _():
        ubuf_ref[0:CARRY_ROWS, :] = ubuf_ref[tm:tm + CARRY_ROWS, :]

    span = tm // ROW_SPANS
    xs = [x_ref[k * span:(k + 1) * span, :] for k in range(ROW_SPANS)]
    hns = [(_rms_unit(x) * gpre_ref[...]).astype(BF16) for x in xs]
    gates = [_dot(hn, win_ref[:, 0:D_MODEL]) for hn in hns]
    us = [_dot(hn, win_ref[:, D_MODEL:2 * D_MODEL]) * _dot(hn, win_ref[:, 2 * D_MODEL:3 * D_MODEL])
          for hn in hns]
    ys = []
    for k, (gate, u) in enumerate(zip(gates, us)):
        base = CARRY_ROWS + k * span
        ubuf_ref[base:base + span, :] = u
        u1 = ubuf_ref[base - 1:base - 1 + span, :]
        u2 = ubuf_ref[base - 2:base - 2 + span, :]
        conv = cw_ref[0:1, :] * u2 + cw_ref[1:2, :] * u1 + cw_ref[2:3, :] * u
        ys.append(_dot((gate * conv).astype(BF16), wout_ref[...]))
    for k, (x, y) in enumerate(zip(xs, ys)):
        o_ref[k * span:(k + 1) * span, :] = x + _rms_unit(y) * gpost_ref[...]


def _conv_mixer(x, g_pre, w_in, conv_w, w_out, g_post, casts):
    batch, seq, d = x.shape
    tm = TOKEN_TILE
    tiles = seq // tm
    row = pl.BlockSpec((None, tm, d), lambda b, j: (b, j, 0))
    c_in, c_out, c_shape = _cast_plan(casts, batch * tiles, lambda b, j: b * tiles + j)
    in_specs = [row, _resident((1, d)), _resident(w_in.shape), _resident(conv_w.shape),
                _resident(w_out.shape), _resident((1, d))]
    outs = pl.pallas_call(
        _with_casts(_conv_mixer_kernel, len(in_specs), 1, len(casts)),
        grid=(batch, tiles),
        in_specs=in_specs + c_in,
        out_specs=[row] + c_out,
        out_shape=[jax.ShapeDtypeStruct(x.shape, F32)] + c_shape,
        scratch_shapes=[pltpu.VMEM((tm + CARRY_ROWS, d), F32)],
        compiler_params=_params("arbitrary", "arbitrary"),
        name="conv_mixer",
    )(x, g_pre, w_in, conv_w, w_out, g_post, *[w for w, _ in casts])
    return outs[0], outs[1:]


def _swiglu_residual(hs, gpre_ref, wgu_ref, wd_ref, gpost_ref, act_ref):
    d_ff = wd_ref.shape[0]
    span = hs[0].shape[0]
    hns = [(_rms_unit(h) * gpre_ref[...]).astype(BF16) for h in hs]
    for c in range(0, d_ff, MXU_COLS):
        for k, hn in enumerate(hns):
            g = _dot(hn, wgu_ref[:, c:c + MXU_COLS])
            u = _dot(hn, wgu_ref[:, d_ff + c:d_ff + c + MXU_COLS])
            act_ref[k * span:(k + 1) * span, c:c + MXU_COLS] = (g * jax.nn.sigmoid(g) * u).astype(BF16)
    fs = [_dot(act_ref[k * span:(k + 1) * span, :], wd_ref[...]) for k in range(len(hs))]
    return [h + _rms_unit(f) * gpost_ref[...] for h, f in zip(hs, fs)]


def _row_spans(ref):
    span = ref.shape[0] // ROW_SPANS
    return [ref[k * span:(k + 1) * span, :] for k in range(ROW_SPANS)]


def _ffn_kernel(h_ref, gpre_ref, wgu_ref, wd_ref, gpost_ref, o_ref, act_ref):
    outs = _swiglu_residual(_row_spans(h_ref), gpre_ref, wgu_ref, wd_ref, gpost_ref, act_ref)
    o_ref[...] = jnp.concatenate(outs, axis=0)


def _attn_out_ffn_kernel(h_ref, a_ref, wo_ref, gmix_ref, gpre_ref, wgu_ref, wd_ref, gpost_ref,
                         o_ref, act_ref):
    hs = [h + _rms_unit(_dot(a, wo_ref[...])) * gmix_ref[...]
          for h, a in zip(_row_spans(h_ref), _row_spans(a_ref))]
    outs = _swiglu_residual(hs, gpre_ref, wgu_ref, wd_ref, gpost_ref, act_ref)
    o_ref[...] = jnp.concatenate(outs, axis=0)


def _ffn(h, g_pre, w_gate_up, w_down, g_post, attn=None, casts=()):
    tokens, d = h.shape
    d_ff = w_down.shape[0]
    assert d_ff % MXU_COLS == 0
    tm = TOKEN_TILE
    row = pl.BlockSpec((tm, d), lambda i: (i, 0))
    ffn_specs = [_resident((1, d)), _resident(w_gate_up.shape), _resident(w_down.shape),
                 _resident((1, d))]
    if attn is None:
        body, specs, args = _ffn_kernel, [row], (h,)
    else:
        a, w_o, g_mix = attn
        body = _attn_out_ffn_kernel
        specs = [row, row, _resident(w_o.shape), _resident((1, d))]
        args = (h, a, w_o, g_mix)
    in_specs = specs + ffn_specs
    c_in, c_out, c_shape = _cast_plan(casts, tokens // tm, lambda i: i)
    outs = pl.pallas_call(
        _with_casts(body, len(in_specs), 1, len(casts)),
        grid=(tokens // tm,),
        in_specs=in_specs + c_in,
        out_specs=[row] + c_out,
        out_shape=[jax.ShapeDtypeStruct(h.shape, F32)] + c_shape,
        scratch_shapes=[pltpu.VMEM((tm, d_ff), BF16)],
        compiler_params=_params("arbitrary"),
        name="ffn" if attn is None else "attn_out_ffn",
    )(*args, g_pre, w_gate_up, w_down, g_post, *[w for w, _ in casts])
    return outs[0], outs[1:]


def _qkv_kernel(h_ref, pos_ref, invf_ref, gkv_ref, gq_ref, wkv_ref, wq_ref, *rest):
    out_refs, (hbuf_ref, trig_ref) = rest[:3 * N_BRANCHES], rest[3 * N_BRANCHES:]
    tm = h_ref.shape[0]

    for s in range(N_SLABS):
        hbuf_ref[s] = h_ref[:, s * LANES:(s + 1) * LANES]

    lane = lax.broadcasted_iota(jnp.int32, (1, LANES), 1)
    half = HEAD_DIM // 2
    groups = LANES // half
    packed = tm // groups
    group = lane // half
    pos = pos_ref[...].astype(F32)
    pos_packed = jnp.broadcast_to(pos[0:packed], (packed, LANES))
    for j in range(1, groups):
        pos_packed = jnp.where(group == j, pos[j * packed:(j + 1) * packed], pos_packed)
    ang = pos_packed * invf_ref[...]
    for t, table in enumerate((jnp.cos(ang), jnp.sin(ang))):
        for j in range(groups):
            x = jnp.where(group == j, table, 0.0)
            shift = LANES // 2
            while shift >= half:
                x = x + pltpu.roll(x, shift, axis=1)
                shift //= 2
            trig_ref[t, j * packed:(j + 1) * packed, :] = x

    first_half = (lane % HEAD_DIM) < (HEAD_DIM // 2)
    scale = HEAD_DIM ** -0.5 * LOG2_E

    for g, dil in enumerate(DILATIONS):
        rows = tm // dil

        def gather(read):
            if dil == 1:
                return read(slice(None))
            return jnp.concatenate([read(pl.ds(r, rows, stride=dil)) for r in range(dil)], axis=0)

        h = gather(lambda idx: jnp.concatenate([hbuf_ref[s, idx, :] for s in range(N_SLABS)], axis=1))
        cos = gather(lambda idx: trig_ref[0, idx, :])
        sin = gather(lambda idx: trig_ref[1, idx, :])
        sin_signed = jnp.where(first_half, -sin, sin)

        def rope(t):
            partner = jnp.where(first_half,
                                pltpu.roll(t, LANES - HEAD_DIM // 2, axis=1),
                                pltpu.roll(t, HEAD_DIM // 2, axis=1))
            return t * cos + partner * sin_signed

        def emit(ref, cols, val):
            if dil == 1:
                ref[:, cols] = val.astype(BF16)
            else:
                for r in range(dil):
                    ref[r, :, cols] = val[r * rows:(r + 1) * rows].astype(BF16)

        y = _rms_unit(h)
        a_kv = (y * gkv_ref[...]).astype(BF16)
        a_q = (y * gq_ref[...]).astype(BF16)
        q_ref, k_ref, v_ref = out_refs[3 * g:3 * g + 3]
        base = g * D_MODEL
        for c in range(0, D_MODEL, MXU_COLS):
            tq = _dot(a_q, wq_ref[:, base + c:base + c + MXU_COLS])
            tk = _dot(a_kv, wkv_ref[:, base + c:base + c + MXU_COLS])
            tv = _dot(a_kv, wkv_ref[:, Q_WIDTH + base + c:Q_WIDTH + base + c + MXU_COLS])
            for s in range(0, MXU_COLS, LANES):
                cols = slice(c + s, c + s + LANES)
                emit(q_ref, cols, rope(tq[:, s:s + LANES]) * scale)
                emit(k_ref, cols, rope(tk[:, s:s + LANES]))
            emit(v_ref, slice(c, c + MXU_COLS), tv)


def _qkv(h, pos, inv_freq, g_kv, g_q, w_kv, w_q):
    tokens, d = h.shape
    tm = TOKEN_TILE
    assert CHUNK % tm == 0 and tokens % CHUNK == 0
    tiles_per_chunk = CHUNK // tm
    n_chunks = tokens // CHUNK
    row = pl.BlockSpec((tm, d), lambda i: (i, 0))
    out_specs, out_shapes = [], []
    for dil in DILATIONS:
        if dil == 1:
            spec = row
            shape = jax.ShapeDtypeStruct((tokens, d), BF16)
        else:
            assert (tm // dil) % BF16_SUBLANES == 0
            spec = pl.BlockSpec((None, dil, tm // dil, d),
                                lambda i: (i // tiles_per_chunk, 0, i % tiles_per_chunk, 0))
            shape = jax.ShapeDtypeStruct((n_chunks, dil, CHUNK // dil, d), BF16)
        out_specs += [spec] * 3
        out_shapes += [shape] * 3
    outs = pl.pallas_call(
        _qkv_kernel,
        grid=(tokens // tm,),
        in_specs=[row, pl.BlockSpec((tm, 1), lambda i: (i, 0)), _resident((1, LANES)),
                  _resident((1, d)), _resident((1, d)), _resident(w_kv.shape), _resident(w_q.shape)],
        out_specs=out_specs,
        out_shape=out_shapes,
        scratch_shapes=[pltpu.VMEM((N_SLABS, tm, LANES), F32), pltpu.VMEM((2, tm, LANES), F32)],
        compiler_params=_params("arbitrary"),
        name="qkv_rope",
    )(h, pos, inv_freq, g_kv, g_q, w_kv, w_q)
    dils = [dil for dil in DILATIONS for _ in range(3)]
    return [o.reshape(n_chunks, dil, CHUNK // dil, d) for o, dil in zip(outs, dils)]


ATTN_OPERANDS = 5


def _attn_kernel(*refs):
    n_in = ATTN_OPERANDS * N_BRANCHES
    in_refs, o_ref, scratch = refs[:n_in], refs[n_in], refs[n_in + 1:]
    stages, cap_ref = (scratch[0:3], scratch[3:6]), scratch[6]
    chunk = pl.program_id(2)

    lane = lax.broadcasted_iota(jnp.int32, (1, LANES), 1)
    low_head = lane < HEAD_DIM
    qi = lax.broadcasted_iota(jnp.int32, (BAND, 2 * BAND), 0)
    kj = lax.broadcasted_iota(jnp.int32, (BAND, 2 * BAND), 1)
    dist = qi + BAND - kj
    in_band = (dist >= 0) & (dist <= BAND)
    first_mask = in_band & ((kj >= BAND) | (chunk > 0))
    cap_ref[0] = jnp.where(in_band, F32_MAX, NEG_INF)
    cap_ref[1] = jnp.where(first_mask, F32_MAX, NEG_INF)
    ones = jnp.ones((2 * BAND, LANES), BF16)
    nt = (((1,), (1,)), ((), ()))

    def block(g, r, n):
        q_ref, k_ref, v_ref, ktail_ref, vtail_ref = in_refs[ATTN_OPERANDS * g:ATTN_OPERANDS * (g + 1)]
        q2 = q_ref[r, n * BAND:(n + 1) * BAND, :]
        if n == 0:
            k2 = jnp.concatenate([ktail_ref[r], k_ref[r, 0:BAND, :]], axis=0)
            v2 = jnp.concatenate([vtail_ref[r], v_ref[r, 0:BAND, :]], axis=0)
        else:
            k2 = k_ref[r, (n - 1) * BAND:(n + 1) * BAND, :]
            v2 = v_ref[r, (n - 1) * BAND:(n + 1) * BAND, :]
        zero = jnp.zeros_like(q2)
        qs = jnp.concatenate([jnp.where(low_head, q2, zero), jnp.where(low_head, zero, q2)], axis=0)
        s = lax.dot_general(qs, k2, nt, preferred_element_type=F32)
        cap = cap_ref[1 if n == 0 else 0]
        s = jnp.minimum(s, jnp.concatenate([cap, cap], axis=0))
        m = jnp.max(s, axis=-1, keepdims=True)
        p = jnp.exp2(s - m).astype(BF16)
        out = _dot(p, jnp.concatenate([v2, ones], axis=1))
        acc = jnp.where(low_head, out[0:BAND, 0:LANES], out[BAND:, 0:LANES])
        l = jnp.where(low_head, out[0:BAND, LANES:], out[BAND:, LANES:])
        m2 = jnp.where(low_head, m[0:BAND], m[BAND:])
        return m2, l, acc

    def merge(a, b):
        top = jnp.maximum(a[0], b[0])
        ea = jnp.exp2(a[0] - top)
        eb = jnp.exp2(b[0] - top)
        return top, ea * a[1] + eb * b[1], ea * a[2] + eb * b[2]

    for g in range(N_BRANCHES - 1, -1, -1):
        dil = DILATIONS[g]
        per_run = CHUNK // dil // BAND
        for idx in range(BLOCKS_PER_CHUNK):
            r, n = divmod(idx, per_run)
            triple = block(g, r, n)
            own_rows = slice(idx * BAND, (idx + 1) * BAND)
            if g < N_BRANCHES - 1:
                triple = merge(triple, [ref[own_rows, :] for ref in stages[(g + 1) % 2]])
            if g > 0:
                inner = DILATIONS[g - 1]
                ratio = dil // inner
                start = (r % inner) * (CHUNK // inner) + r // inner + n * BAND * ratio
                for ref, val in zip(stages[g % 2], triple):
                    ref[pl.ds(start, BAND, stride=ratio), :] = val
            else:
                o_ref[own_rows, :] = (triple[2] / triple[1]).astype(o_ref.dtype)


def _attention(qkv, batch, seq, casts):
    chunks_per_seq = seq // CHUNK

    def here(b, j, c):
        return (b * chunks_per_seq + c, 0, 0, j)

    def tail_of_previous(b, j, c, last_block):
        return (b * chunks_per_seq + jnp.maximum(c - 1, 0), 0, last_block, j)

    in_specs, operands = [], []
    for g, dil in enumerate(DILATIONS):
        run = CHUNK // dil
        cur = pl.BlockSpec((None, dil, run, LANES), here)
        tail = pl.BlockSpec((None, dil, BAND, LANES),
                            lambda b, j, c, last=run // BAND - 1: tail_of_previous(b, j, c, last))
        q, k, v = qkv[3 * g:3 * g + 3]
        in_specs += [cur, cur, cur, tail, tail]
        operands += [q, k, v, k, v]
    scratch = [pltpu.VMEM((CHUNK, LANES), F32)] * 6 + [pltpu.VMEM((2, BAND, 2 * BAND), F32)]
    steps = batch * N_HEAD_PAIRS * chunks_per_seq
    c_in, c_out, c_shape = _cast_plan(
        casts, steps, lambda b, j, c: (b * N_HEAD_PAIRS + j) * chunks_per_seq + c)
    n_in = len(in_specs)
    outs = pl.pallas_call(
        _with_casts(_attn_kernel, n_in, 1, len(casts)),
        grid=(batch, N_HEAD_PAIRS, chunks_per_seq),
        in_specs=in_specs + c_in,
        out_specs=[pl.BlockSpec((CHUNK, LANES), lambda b, j, c: (b * chunks_per_seq + c, j))] + c_out,
        out_shape=[jax.ShapeDtypeStruct((batch * seq, D_MODEL), BF16)] + c_shape,
        scratch_shapes=scratch,
        compiler_params=_params("arbitrary", "arbitrary", "arbitrary"),
        name="dilated_attn",
    )(*operands, *[w for w, _ in casts])
    return outs[0], outs[1:]


def kernel(x, positions, mix_norm_pre, mix_norm_post, ffn_norm_pre, ffn_norm_post, ffn_w_gate_up,
           ffn_w_down, conv_w_in, conv_w, conv_w_out, kv_norm, w_kv, w_q, w_o):
    batch, seq, d = x.shape
    tokens = batch * seq
    for window, dilation in BRANCHES:
        assert window // dilation == BAND and seq % (BAND * dilation) == 0

    def gain(g):
        return g.reshape(1, d).astype(F32)

    half = HEAD_DIM // 2
    inv_freq = ROPE_THETA ** (-jnp.arange(half, dtype=F32) / half)
    inv_freq = jnp.tile(inv_freq, LANES // half).reshape(1, LANES)
    pos = positions.reshape(tokens, 1)

    h, (w_gu0, w_dn0) = _conv_mixer(
        x, gain(mix_norm_pre[0]), conv_w_in[0].astype(BF16), conv_w[0], conv_w_out[0].astype(BF16),
        gain(mix_norm_post[0]), casts=[(ffn_w_gate_up, 0), (ffn_w_down, 0)])
    h = h.reshape(tokens, d)
    h, (w_kv16, w_q16) = _ffn(h, gain(ffn_norm_pre[0]), w_gu0, w_dn0, gain(ffn_norm_post[0]),
                              casts=[(w_kv[None], 0), (w_q, 0)])

    qkv = _qkv(h, pos, inv_freq, gain(kv_norm), gain(mix_norm_pre[1]), w_kv16, w_q16)
    a, (w_gu1, w_dn1, w_o16) = _attention(
        qkv, batch, seq, casts=[(ffn_w_gate_up, 1), (ffn_w_down, 1), (w_o, 0)])
    h, _ = _ffn(h, gain(ffn_norm_pre[1]), w_gu1, w_dn1, gain(ffn_norm_post[1]),
                attn=(a, w_o16, gain(mix_norm_post[1])))
    return h.reshape(batch, seq, d)
```

```python
import math

import jax
import jax.numpy as jnp
from jax import lax
from jax.experimental import pallas as pl
from jax.experimental.pallas import tpu as pltpu

D_MODEL = 1024
HEAD_DIM = 64
N_HEADS = D_MODEL // HEAD_DIM
BRANCHES = ((128, 1), (512, 4), (2048, 16))
DILATIONS = tuple(d for _, d in BRANCHES)
N_BRANCHES = len(BRANCHES)
Q_WIDTH = N_BRANCHES * N_HEADS * HEAD_DIM
CONV_WIDTH = 3
ROPE_THETA = 10000.0
RMS_EPS = 1e-6
NEG_INF = -1e30
LOG2_E = math.log2(math.e)
F32_MAX = 3.0e38

LANES = 128
BF16_SUBLANES = 16
MXU_COLS = 256
BAND = 128
CHUNK = BAND * max(DILATIONS)
BLOCKS_PER_CHUNK = CHUNK // BAND
HEADS_PER_VREG = LANES // HEAD_DIM
N_HEAD_PAIRS = N_HEADS // HEADS_PER_VREG
N_SLABS = D_MODEL // LANES
VMEM_LIMIT_BYTES = 56 * 1024 * 1024

TOKEN_TILE = 512
ROW_SPANS = 2
CARRY_ROWS = 8

F32 = jnp.float32
BF16 = jnp.bfloat16


def _dot(a, b):
    return jnp.dot(a, b, preferred_element_type=F32)


def _rms_unit(x):
    return x * lax.rsqrt(jnp.mean(x * x, axis=-1, keepdims=True) + RMS_EPS)


def _resident(shape):
    return pl.BlockSpec(shape, lambda *_: (0,) * len(shape), pipeline_mode=pl.Buffered(1))


def _params(*semantics):
    return pltpu.CompilerParams(dimension_semantics=semantics, vmem_limit_bytes=VMEM_LIMIT_BYTES)


def _cast_plan(weights, steps, linear_step):
    in_specs, out_specs, out_shapes = [], [], []
    for w, layer in weights:
        _, rows, cols = w.shape
        block = -(-rows // steps)
        block = -(-block // BF16_SUBLANES) * BF16_SUBLANES
        while rows % block:
            block += BF16_SUBLANES
        last = rows // block - 1

        def src(*idx, layer=layer, last=last):
            return (layer, jnp.minimum(linear_step(*idx), last), 0)

        def dst(*idx, last=last):
            return (jnp.minimum(linear_step(*idx), last), 0)

        in_specs.append(pl.BlockSpec((None, block, cols), src))
        out_specs.append(pl.BlockSpec((block, cols), dst))
        out_shapes.append(jax.ShapeDtypeStruct((rows, cols), BF16))
    return in_specs, out_specs, out_shapes


def _with_casts(body, n_in, n_out, n_cast):
    def kernel(*refs):
        ins, refs = refs[:n_in], refs[n_in:]
        cast_src, refs = refs[:n_cast], refs[n_cast:]
        outs, refs = refs[:n_out], refs[n_out:]
        cast_dst, scratch = refs[:n_cast], refs[n_cast:]
        body(*ins, *outs, *scratch)
        for s_ref, d_ref in zip(cast_src, cast_dst):
            d_ref[...] = s_ref[...].astype(BF16)
    return kernel


def _conv_mixer_kernel(x_ref, gpre_ref, win_ref, cw_ref, wout_ref, gpost_ref, o_ref, ubuf_ref):
    j = pl.program_id(1)
    tm = x_ref.shape[0]

    @pl.when(j == 0)
    def _():
        ubuf_ref[0:CARRY_ROWS, :] = jnp.zeros((CARRY_ROWS, D_MODEL), F32)

    @pl.when(j > 0)
    def _():
        ubuf_ref[0:CARRY_ROWS, :] = ubuf_ref[tm:tm + CARRY_ROWS, :]

    span = tm // ROW_SPANS
    xs = [x_ref[k * span:(k + 1) * span, :] for k in range(ROW_SPANS)]
    hns = [(_rms_unit(x) * gpre_ref[...]).astype(BF16) for x in xs]
    gates = [_dot(hn, win_ref[:, 0:D_MODEL]) for hn in hns]
    us = [_dot(hn, win_ref[:, D_MODEL:2 * D_MODEL]) * _dot(hn, win_ref[:, 2 * D_MODEL:3 * D_MODEL])
          for hn in hns]
    ys = []
    for k, (gate, u) in enumerate(zip(gates, us)):
        base = CARRY_ROWS + k * span
        ubuf_ref[base:base + span, :] = u
        u1 = ubuf_ref[base - 1:base - 1 + span, :]
        u2 = ubuf_ref[base - 2:base - 2 + span, :]
        conv = cw_ref[0:1, :] * u2 + cw_ref[1:2, :] * u1 + cw_ref[2:3, :] * u
        ys.append(_dot((gate * conv).astype(BF16), wout_ref[...]))
    for k, (x, y) in enumerate(zip(xs, ys)):
        o_ref[k * span:(k + 1) * span, :] = x + _rms_unit(y) * gpost_ref[...]


def _conv_mixer(x, g_pre, w_in, conv_w, w_out, g_post, casts):
    batch, seq, d = x.shape
    tm = TOKEN_TILE
    tiles = seq // tm
    row = pl.BlockSpec((None, tm, d), lambda b, j: (b, j, 0))
    c_in, c_out, c_shape = _cast_plan(casts, batch * tiles, lambda b, j: b * tiles + j)
    in_specs = [row, _resident((1, d)), _resident(w_in.shape), _resident(conv_w.shape),
                _resident(w_out.shape), _resident((1, d))]
    outs = pl.pallas_call(
        _with_casts(_conv_mixer_kernel, len(in_specs), 1, len(casts)),
        grid=(batch, tiles),
        in_specs=in_specs + c_in,
        out_specs=[row] + c_out,
        out_shape=[jax.ShapeDtypeStruct(x.shape, F32)] + c_shape,
        scratch_shapes=[pltpu.VMEM((tm + CARRY_ROWS, d), F32)],
        compiler_params=_params("arbitrary", "arbitrary"),
        name="conv_mixer",
    )(x, g_pre, w_in, conv_w, w_out, g_post, *[w for w, _ in casts])
    return outs[0], outs[1:]


def _swiglu_residual(hs, gpre_ref, wgu_ref, wd_ref, gpost_ref, act_ref):
    d_ff = wd_ref.shape[0]
    span = hs[0].shape[0]
    hns = [(_rms_unit(h) * gpre_ref[...]).astype(BF16) for h in hs]
    for c in range(0, d_ff, MXU_COLS):
        for k, hn in enumerate(hns):
            g = _dot(hn, wgu_ref[:, c:c + MXU_COLS])
            u = _dot(hn, wgu_ref[:, d_ff + c:d_ff + c + MXU_COLS])
            act_ref[k * span:(k + 1) * span, c:c + MXU_COLS] = (g * jax.nn.sigmoid(g) * u).astype(BF16)
    fs = [_dot(act_ref[k * span:(k + 1) * span, :], wd_ref[...]) for k in range(len(hs))]
    return [h + _rms_unit(f) * gpost_ref[...] for h, f in zip(hs, fs)]


def _row_spans(ref):
    span = ref.shape[0] // ROW_SPANS
    return [ref[k * span:(k + 1) * span, :] for k in range(ROW_SPANS)]


def _ffn_kernel(h_ref, gpre_ref, wgu_ref, wd_ref, gpost_ref, o_ref, act_ref):
    outs = _swiglu_residual(_row_spans(h_ref), gpre_ref, wgu_ref, wd_ref, gpost_ref, act_ref)
    o_ref[...] = jnp.concatenate(outs, axis=0)


def _attn_out_ffn_kernel(h_ref, a_ref, wo_ref, gmix_ref, gpre_ref, wgu_ref, wd_ref, gpost_ref,
                         o_ref, act_ref):
    hs = [h + _rms_unit(_dot(a, wo_ref[...])) * gmix_ref[...]
          for h, a in zip(_row_spans(h_ref), _row_spans(a_ref))]
    outs = _swiglu_residual(hs, gpre_ref, wgu_ref, wd_ref, gpost_ref, act_ref)
    o_ref[...] = jnp.concatenate(outs, axis=0)


def _ffn(h, g_pre, w_gate_up, w_down, g_post, attn=None, casts=()):
    tokens, d = h.shape
    d_ff = w_down.shape[0]
    assert d_ff % MXU_COLS == 0
    tm = TOKEN_TILE
    row = pl.BlockSpec((tm, d), lambda i: (i, 0))
    ffn_specs = [_resident((1, d)), _resident(w_gate_up.shape), _resident(w_down.shape),
                 _resident((1, d))]
    if attn is None:
        body, specs, args = _ffn_kernel, [row], (h,)
    else:
        a, w_o, g_mix = attn
        body = _attn_out_ffn_kernel
        specs = [row, row, _resident(w_o.shape), _resident((1, d))]
        args = (h, a, w_o, g_mix)
    in_specs = specs + ffn_specs
    c_in, c_out, c_shape = _cast_plan(casts, tokens // tm, lambda i: i)
    outs = pl.pallas_call(
        _with_casts(body, len(in_specs), 1, len(casts)),
        grid=(tokens // tm,),
        in_specs=in_specs + c_in,
        out_specs=[row] + c_out,
        out_shape=[jax.ShapeDtypeStruct(h.shape, F32)] + c_shape,
        scratch_shapes=[pltpu.VMEM((tm, d_ff), BF16)],
        compiler_params=_params("arbitrary"),
        name="ffn" if attn is None else "attn_out_ffn",
    )(*args, g_pre, w_gate_up, w_down, g_post, *[w for w, _ in casts])
    return outs[0], outs[1:]


def _qkv_kernel(h_ref, pos_ref, invf_ref, gkv_ref, gq_ref, wkv_ref, wq_ref, *rest):
    out_refs, stage_ref = rest[:3 * N_BRANCHES], rest[3 * N_BRANCHES]
    tm = h_ref.shape[0]

    for s in range(N_SLABS):
        stage_ref[0, s] = h_ref[:, s * LANES:(s + 1) * LANES]

    lane = lax.broadcasted_iota(jnp.int32, (1, LANES), 1)
    half = HEAD_DIM // 2
    groups = LANES // half
    packed = tm // groups
    group = lane // half
    pos = pos_ref[...].astype(F32)
    pos_packed = jnp.broadcast_to(pos[0:packed], (packed, LANES))
    for j in range(1, groups):
        pos_packed = jnp.where(group == j, pos[j * packed:(j + 1) * packed], pos_packed)
    ang = pos_packed * invf_ref[...]
    for t, table in enumerate((jnp.cos(ang), jnp.sin(ang))):
        for j in range(groups):
            x = jnp.where(group == j, table, 0.0)
            shift = LANES // 2
            while shift >= half:
                x = x + pltpu.roll(x, shift, axis=1)
                shift //= 2
            stage_ref[0, N_SLABS + t, j * packed:(j + 1) * packed, :] = x

    first_half = (lane % HEAD_DIM) < (HEAD_DIM // 2)
    scale = HEAD_DIM ** -0.5 * LOG2_E

    prepared = []
    for g, dil in enumerate(DILATIONS):
        rows = tm // dil
        src = stage_ref.at[(g - 1) % 2] if g else stage_ref.at[0]
        if g == 0:
            slabs = [src[s] for s in range(N_SLABS + 2)]
        else:
            inner = DILATIONS[g - 1]
            ratio = dil // inner
            starts = [(r % inner) * (tm // inner) + r // inner for r in range(dil)]
            slabs = [jnp.concatenate([src[s, pl.ds(st, rows, stride=ratio), :] for st in starts], axis=0)
                     for s in range(N_SLABS + 2)]
            if g < N_BRANCHES - 1:
                for s, slab in enumerate(slabs):
                    stage_ref[g % 2, s] = slab
        y = _rms_unit(jnp.concatenate(slabs[:N_SLABS], axis=1))
        cos, sin = slabs[N_SLABS:]
        prepared.append(((y * gq_ref[...]).astype(BF16), (y * gkv_ref[...]).astype(BF16),
                         cos, jnp.where(first_half, -sin, sin)))

    for g, dil in enumerate(DILATIONS):
        rows = tm // dil
        a_q, a_kv, cos, sin_signed = prepared[g]

        def rope(t):
            partner = jnp.where(first_half,
                                pltpu.roll(t, LANES - HEAD_DIM // 2, axis=1),
                                pltpu.roll(t, HEAD_DIM // 2, axis=1))
            return t * cos + partner * sin_signed

        def emit(ref, cols, val):
            if dil == 1:
                ref[:, cols] = val.astype(BF16)
            else:
                for r in range(dil):
                    ref[r, :, cols] = val[r * rows:(r + 1) * rows].astype(BF16)

        q_ref, k_ref, v_ref = out_refs[3 * g:3 * g + 3]
        base = g * D_MODEL
        for c in range(0, D_MODEL, MXU_COLS):
            tq = _dot(a_q, wq_ref[:, base + c:base + c + MXU_COLS])
            tk = _dot(a_kv, wkv_ref[:, base + c:base + c + MXU_COLS])
            tv = _dot(a_kv, wkv_ref[:, Q_WIDTH + base + c:Q_WIDTH + base + c + MXU_COLS])
            for s in range(0, MXU_COLS, LANES):
                cols = slice(c + s, c + s + LANES)
                emit(q_ref, cols, rope(tq[:, s:s + LANES]) * scale)
                emit(k_ref, cols, rope(tk[:, s:s + LANES]))
            emit(v_ref, slice(c, c + MXU_COLS), tv)


def _qkv(h, pos, inv_freq, g_kv, g_q, w_kv, w_q):
    tokens, d = h.shape
    tm = TOKEN_TILE
    assert CHUNK % tm == 0 and tokens % CHUNK == 0
    tiles_per_chunk = CHUNK // tm
    n_chunks = tokens // CHUNK
    row = pl.BlockSpec((tm, d), lambda i: (i, 0))
    out_specs, out_shapes = [], []
    for dil in DILATIONS:
        if dil == 1:
            spec = row
            shape = jax.ShapeDtypeStruct((tokens, d), BF16)
        else:
            assert (tm // dil) % BF16_SUBLANES == 0
            spec = pl.BlockSpec((None, dil, tm // dil, d),
                                lambda i: (i // tiles_per_chunk, 0, i % tiles_per_chunk, 0))
            shape = jax.ShapeDtypeStruct((n_chunks, dil, CHUNK // dil, d), BF16)
        out_specs += [spec] * 3
        out_shapes += [shape] * 3
    outs = pl.pallas_call(
        _qkv_kernel,
        grid=(tokens // tm,),
        in_specs=[row, pl.BlockSpec((tm, 1), lambda i: (i, 0)), _resident((1, LANES)),
                  _resident((1, d)), _resident((1, d)), _resident(w_kv.shape), _resident(w_q.shape)],
        out_specs=out_specs,
        out_shape=out_shapes,
        scratch_shapes=[pltpu.VMEM((2, N_SLABS + 2, tm, LANES), F32)],
        compiler_params=_params("arbitrary"),
        name="qkv_rope",
    )(h, pos, inv_freq, g_kv, g_q, w_kv, w_q)
    dils = [dil for dil in DILATIONS for _ in range(3)]
    return [o.reshape(n_chunks, dil, CHUNK // dil, d) for o, dil in zip(outs, dils)]


ATTN_OPERANDS = 5


def _attn_kernel(*refs):
    n_in = ATTN_OPERANDS * N_BRANCHES
    in_refs, o_ref, scratch = refs[:n_in], refs[n_in], refs[n_in + 1:]
    stages, cap_ref = (scratch[0:3], scratch[3:6]), scratch[6]
    chunk = pl.program_id(2)

    lane = lax.broadcasted_iota(jnp.int32, (1, LANES), 1)
    low_head = lane < HEAD_DIM
    qi = lax.broadcasted_iota(jnp.int32, (BAND, 2 * BAND), 0)
    kj = lax.broadcasted_iota(jnp.int32, (BAND, 2 * BAND), 1)
    dist = qi + BAND - kj
    in_band = (dist >= 0) & (dist <= BAND)
    first_mask = in_band & ((kj >= BAND) | (chunk > 0))
    cap_ref[0] = jnp.where(in_band, F32_MAX, NEG_INF)
    cap_ref[1] = jnp.where(first_mask, F32_MAX, NEG_INF)
    ones = jnp.ones((2 * BAND, LANES), BF16)
    nt = (((1,), (1,)), ((), ()))

    def block(g, r, n):
        q_ref, k_ref, v_ref, ktail_ref, vtail_ref = in_refs[ATTN_OPERANDS * g:ATTN_OPERANDS * (g + 1)]
        q2 = q_ref[r, n * BAND:(n + 1) * BAND, :]
        if n == 0:
            k2 = jnp.concatenate([ktail_ref[r], k_ref[r, 0:BAND, :]], axis=0)
            v2 = jnp.concatenate([vtail_ref[r], v_ref[r, 0:BAND, :]], axis=0)
        else:
            k2 = k_ref[r, (n - 1) * BAND:(n + 1) * BAND, :]
            v2 = v_ref[r, (n - 1) * BAND:(n + 1) * BAND, :]
        zero = jnp.zeros_like(q2)
        qs = jnp.concatenate([jnp.where(low_head, q2, zero), jnp.where(low_head, zero, q2)], axis=0)
        s = lax.dot_general(qs, k2, nt, preferred_element_type=F32)
        cap = cap_ref[1 if n == 0 else 0]
        s = jnp.minimum(s, jnp.concatenate([cap, cap], axis=0))
        m = jnp.max(s, axis=-1, keepdims=True)
        p = jnp.exp2(s - m).astype(BF16)
        out = _dot(p, jnp.concatenate([v2, ones], axis=1))
        acc = jnp.where(low_head, out[0:BAND, 0:LANES], out[BAND:, 0:LANES])
        l = jnp.where(low_head, out[0:BAND, LANES:], out[BAND:, LANES:])
        m2 = jnp.where(low_head, m[0:BAND], m[BAND:])
        return m2, l, acc

    def merge(a, b):
        top = jnp.maximum(a[0], b[0])
        ea = jnp.exp2(a[0] - top)
        eb = jnp.exp2(b[0] - top)
        return top, ea * a[1] + eb * b[1], ea * a[2] + eb * b[2]

    for g in range(N_BRANCHES - 1, -1, -1):
        dil = DILATIONS[g]
        per_run = CHUNK // dil // BAND
        for idx in range(BLOCKS_PER_CHUNK):
            r, n = divmod(idx, per_run)
            triple = block(g, r, n)
            own_rows = slice(idx * BAND, (idx + 1) * BAND)
            if g < N_BRANCHES - 1:
                triple = merge(triple, [ref[own_rows, :] for ref in stages[(g + 1) % 2]])
            if g > 0:
                inner = DILATIONS[g - 1]
                ratio = dil // inner
                start = (r % inner) * (CHUNK // inner) + r // inner + n * BAND * ratio
                for ref, val in zip(stages[g % 2], triple):
                    ref[pl.ds(start, BAND, stride=ratio), :] = val
            else:
                o_ref[own_rows, :] = (triple[2] / triple[1]).astype(o_ref.dtype)


def _attention(qkv, batch, seq, casts):
    chunks_per_seq = seq // CHUNK

    def here(b, j, c):
        return (b * chunks_per_seq + c, 0, 0, j)

    def tail_of_previous(b, j, c, last_block):
        return (b * chunks_per_seq + jnp.maximum(c - 1, 0), 0, last_block, j)

    in_specs, operands = [], []
    for g, dil in enumerate(DILATIONS):
        run = CHUNK // dil
        cur = pl.BlockSpec((None, dil, run, LANES), here)
        tail = pl.BlockSpec((None, dil, BAND, LANES),
                            lambda b, j, c, last=run // BAND - 1: tail_of_previous(b, j, c, last))
        q, k, v = qkv[3 * g:3 * g + 3]
        in_specs += [cur, cur, cur, tail, tail]
        operands += [q, k, v, k, v]
    scratch = [pltpu.VMEM((CHUNK, LANES), F32)] * 6 + [pltpu.VMEM((2, BAND, 2 * BAND), F32)]
    steps = batch * N_HEAD_PAIRS * chunks_per_seq
    c_in, c_out, c_shape = _cast_plan(
        casts, steps, lambda b, j, c: (b * N_HEAD_PAIRS + j) * chunks_per_seq + c)
    n_in = len(in_specs)
    outs = pl.pallas_call(
        _with_casts(_attn_kernel, n_in, 1, len(casts)),
        grid=(batch, N_HEAD_PAIRS, chunks_per_seq),
        in_specs=in_specs + c_in,
        out_specs=[pl.BlockSpec((CHUNK, LANES), lambda b, j, c: (b * chunks_per_seq + c, j))] + c_out,
        out_shape=[jax.ShapeDtypeStruct((batch * seq, D_MODEL), BF16)] + c_shape,
        scratch_shapes=scratch,
        compiler_params=_params("arbitrary", "arbitrary", "arbitrary"),
        name="dilated_attn",
    )(*operands, *[w for w, _ in casts])
    return outs[0], outs[1:]


def kernel(x, positions, mix_norm_pre, mix_norm_post, ffn_norm_pre, ffn_norm_post, ffn_w_gate_up,
           ffn_w_down, conv_w_in, conv_w, conv_w_out, kv_norm, w_kv, w_q, w_o):
    batch, seq, d = x.shape
    tokens = batch * seq
    for window, dilation in BRANCHES:
        assert window // dilation == BAND and seq % (BAND * dilation) == 0

    def gain(g):
        return g.reshape(1, d).astype(F32)

    half = HEAD_DIM // 2
    inv_freq = ROPE_THETA ** (-jnp.arange(half, dtype=F32) / half)
    inv_freq = jnp.tile(inv_freq, LANES // half).reshape(1, LANES)
    pos = positions.reshape(tokens, 1)

    h, (w_gu0, w_dn0) = _conv_mixer(
        x, gain(mix_norm_pre[0]), conv_w_in[0].astype(BF16), conv_w[0], conv_w_out[0].astype(BF16),
        gain(mix_norm_post[0]), casts=[(ffn_w_gate_up, 0), (ffn_w_down, 0)])
    h = h.reshape(tokens, d)
    h, (w_kv16, w_q16) = _ffn(h, gain(ffn_norm_pre[0]), w_gu0, w_dn0, gain(ffn_norm_post[0]),
                              casts=[(w_kv[None], 0), (w_q, 0)])

    qkv = _qkv(h, pos, inv_freq, gain(kv_norm), gain(mix_norm_pre[1]), w_kv16, w_q16)
    a, (w_gu1, w_dn1, w_o16) = _attention(
        qkv, batch, seq, casts=[(ffn_w_gate_up, 1), (ffn_w_down, 1), (w_o, 0)])
    h, _ = _ffn(h, gain(ffn_norm_pre[1]), w_gu1, w_dn1, gain(ffn_norm_post[1]),
                attn=(a, w_o16, gain(mix_norm_post[1])))
    return h.reshape(batch, seq, d)
```

```python
import math

import jax
import jax.numpy as jnp
from jax import lax
from jax.experimental import pallas as pl
from jax.experimental.pallas import tpu as pltpu

D_MODEL = 1024
HEAD_DIM = 64
N_HEADS = D_MODEL // HEAD_DIM
BRANCHES = ((128, 1), (512, 4), (2048, 16))
DILATIONS = tuple(d for _, d in BRANCHES)
N_BRANCHES = len(BRANCHES)
Q_WIDTH = N_BRANCHES * N_HEADS * HEAD_DIM
CONV_WIDTH = 3
ROPE_THETA = 10000.0
RMS_EPS = 1e-6
NEG_INF = -1e30
LOG2_E = math.log2(math.e)
F32_MAX = 3.0e38

LANES = 128
BF16_SUBLANES = 16
MXU_COLS = 256
BAND = 128
CHUNK = BAND * max(DILATIONS)
BLOCKS_PER_CHUNK = CHUNK // BAND
HEADS_PER_VREG = LANES // HEAD_DIM
N_HEAD_PAIRS = N_HEADS // HEADS_PER_VREG
N_SLABS = D_MODEL // LANES
VMEM_LIMIT_BYTES = 56 * 1024 * 1024

TOKEN_TILE = 1024
QKV_TILE = 512
ROW_SPANS = 2
PAIRS_PER_STEP = 2
CARRY_ROWS = 8

F32 = jnp.float32
BF16 = jnp.bfloat16


def _dot(a, b):
    return jnp.dot(a, b, preferred_element_type=F32)


def _rms_unit(x):
    return x * lax.rsqrt(jnp.mean(x * x, axis=-1, keepdims=True) + RMS_EPS)


def _resident(shape):
    return pl.BlockSpec(shape, lambda *_: (0,) * len(shape), pipeline_mode=pl.Buffered(1))


def _params(*semantics):
    return pltpu.CompilerParams(dimension_semantics=semantics, vmem_limit_bytes=VMEM_LIMIT_BYTES)


def _cast_plan(weights, steps, linear_step):
    in_specs, out_specs, out_shapes = [], [], []
    for w, layer in weights:
        _, rows, cols = w.shape
        block = -(-rows // steps)
        block = -(-block // BF16_SUBLANES) * BF16_SUBLANES
        while rows % block:
            block += BF16_SUBLANES
        last = rows // block - 1

        def src(*idx, layer=layer, last=last):
            return (layer, jnp.minimum(linear_step(*idx), last), 0)

        def dst(*idx, last=last):
            return (jnp.minimum(linear_step(*idx), last), 0)

        in_specs.append(pl.BlockSpec((None, block, cols), src))
        out_specs.append(pl.BlockSpec((block, cols), dst))
        out_shapes.append(jax.ShapeDtypeStruct((rows, cols), BF16))
    return in_specs, out_specs, out_shapes


def _with_casts(body, n_in, n_out, n_cast):
    def kernel(*refs):
        ins, refs = refs[:n_in], refs[n_in:]
        cast_src, refs = refs[:n_cast], refs[n_cast:]
        outs, refs = refs[:n_out], refs[n_out:]
        cast_dst, scratch = refs[:n_cast], refs[n_cast:]
        body(*ins, *outs, *scratch)
        for s_ref, d_ref in zip(cast_src, cast_dst):
            d_ref[...] = s_ref[...].astype(BF16)
    return kernel


def _conv_mixer_kernel(x_ref, gpre_ref, win_ref, cw_ref, wout_ref, gpost_ref, o_ref, ubuf_ref):
    j = pl.program_id(1)
    tm = x_ref.shape[0]

    @pl.when(j == 0)
    def _():
        ubuf_ref[0:CARRY_ROWS, :] = jnp.zeros((CARRY_ROWS, D_MODEL), F32)

    @pl.when(j > 0)
    def _():
        ubuf_ref[0:CARRY_ROWS, :] = ubuf_ref[tm:tm + CARRY_ROWS, :]

    span = tm // ROW_SPANS
    xs = [x_ref[k * span:(k + 1) * span, :] for k in range(ROW_SPANS)]
    hns = [(_rms_unit(x) * gpre_ref[...]).astype(BF16) for x in xs]
    gates = [_dot(hn, win_ref[:, 0:D_MODEL]) for hn in hns]
    us = [_dot(hn, win_ref[:, D_MODEL:2 * D_MODEL]) * _dot(hn, win_ref[:, 2 * D_MODEL:3 * D_MODEL])
          for hn in hns]
    ys = []
    for k, (gate, u) in enumerate(zip(gates, us)):
        base = CARRY_ROWS + k * span
        ubuf_ref[base:base + span, :] = u
        u1 = ubuf_ref[base - 1:base - 1 + span, :]
        u2 = ubuf_ref[base - 2:base - 2 + span, :]
        conv = cw_ref[0:1, :] * u2 + cw_ref[1:2, :] * u1 + cw_ref[2:3, :] * u
        ys.append(_dot((gate * conv).astype(BF16), wout_ref[...]))
    for k, (x, y) in enumerate(zip(xs, ys)):
        o_ref[k * span:(k + 1) * span, :] = x + _rms_unit(y) * gpost_ref[...]


def _conv_mixer(x, g_pre, w_in, conv_w, w_out, g_post, casts):
    batch, seq, d = x.shape
    tm = TOKEN_TILE
    tiles = seq // tm
    row = pl.BlockSpec((None, tm, d), lambda b, j: (b, j, 0))
    c_in, c_out, c_shape = _cast_plan(casts, batch * tiles, lambda b, j: b * tiles + j)
    in_specs = [row, _resident((1, d)), _resident(w_in.shape), _resident(conv_w.shape),
                _resident(w_out.shape), _resident((1, d))]
    outs = pl.pallas_call(
        _with_casts(_conv_mixer_kernel, len(in_specs), 1, len(casts)),
        grid=(batch, tiles),
        in_specs=in_specs + c_in,
        out_specs=[row] + c_out,
        out_shape=[jax.ShapeDtypeStruct(x.shape, F32)] + c_shape,
        scratch_shapes=[pltpu.VMEM((tm + CARRY_ROWS, d), F32)],
        compiler_params=_params("arbitrary", "arbitrary"),
        name="conv_mixer",
    )(x, g_pre, w_in, conv_w, w_out, g_post, *[w for w, _ in casts])
    return outs[0], outs[1:]


def _swiglu_residual(hs, gpre_ref, wgu_ref, wd_ref, gpost_ref, act_ref):
    d_ff = wd_ref.shape[0]
    span = hs[0].shape[0]
    hns = [(_rms_unit(h) * gpre_ref[...]).astype(BF16) for h in hs]
    for c in range(0, d_ff, MXU_COLS):
        for k, hn in enumerate(hns):
            g = _dot(hn, wgu_ref[:, c:c + MXU_COLS])
            u = _dot(hn, wgu_ref[:, d_ff + c:d_ff + c + MXU_COLS])
            act_ref[k * span:(k + 1) * span, c:c + MXU_COLS] = (g * jax.nn.sigmoid(g) * u).astype(BF16)
    fs = [_dot(act_ref[k * span:(k + 1) * span, :], wd_ref[...]) for k in range(len(hs))]
    return [h + _rms_unit(f) * gpost_ref[...] for h, f in zip(hs, fs)]


def _row_spans(ref):
    span = ref.shape[0] // ROW_SPANS
    return [ref[k * span:(k + 1) * span, :] for k in range(ROW_SPANS)]


def _ffn_kernel(h_ref, gpre_ref, wgu_ref, wd_ref, gpost_ref, o_ref, act_ref):
    outs = _swiglu_residual(_row_spans(h_ref), gpre_ref, wgu_ref, wd_ref, gpost_ref, act_ref)
    o_ref[...] = jnp.concatenate(outs, axis=0)


def _attn_out_ffn_kernel(h_ref, a_ref, wo_ref, gmix_ref, gpre_ref, wgu_ref, wd_ref, gpost_ref,
                         o_ref, act_ref):
    hs = [h + _rms_unit(_dot(a, wo_ref[...])) * gmix_ref[...]
          for h, a in zip(_row_spans(h_ref), _row_spans(a_ref))]
    outs = _swiglu_residual(hs, gpre_ref, wgu_ref, wd_ref, gpost_ref, act_ref)
    o_ref[...] = jnp.concatenate(outs, axis=0)


def _ffn(h, g_pre, w_gate_up, w_down, g_post, attn=None, casts=()):
    tokens, d = h.shape
    d_ff = w_down.shape[0]
    assert d_ff % MXU_COLS == 0
    tm = TOKEN_TILE
    row = pl.BlockSpec((tm, d), lambda i: (i, 0))
    ffn_specs = [_resident((1, d)), _resident(w_gate_up.shape), _resident(w_down.shape),
                 _resident((1, d))]
    if attn is None:
        body, specs, args = _ffn_kernel, [row], (h,)
    else:
        a, w_o, g_mix = attn
        body = _attn_out_ffn_kernel
        specs = [row, row, _resident(w_o.shape), _resident((1, d))]
        args = (h, a, w_o, g_mix)
    in_specs = specs + ffn_specs
    c_in, c_out, c_shape = _cast_plan(casts, tokens // tm, lambda i: i)
    outs = pl.pallas_call(
        _with_casts(body, len(in_specs), 1, len(casts)),
        grid=(tokens // tm,),
        in_specs=in_specs + c_in,
        out_specs=[row] + c_out,
        out_shape=[jax.ShapeDtypeStruct(h.shape, F32)] + c_shape,
        scratch_shapes=[pltpu.VMEM((tm, d_ff), BF16)],
        compiler_params=_params("arbitrary"),
        name="ffn" if attn is None else "attn_out_ffn",
    )(*args, g_pre, w_gate_up, w_down, g_post, *[w for w, _ in casts])
    return outs[0], outs[1:]


def _qkv_kernel(h_ref, pos_ref, invf_ref, gkv_ref, gq_ref, wkv_ref, wq_ref, *rest):
    out_refs, stage_ref = rest[:3 * N_BRANCHES], rest[3 * N_BRANCHES]
    tm = h_ref.shape[0]

    for s in range(N_SLABS):
        stage_ref[0, s] = h_ref[:, s * LANES:(s + 1) * LANES]

    lane = lax.broadcasted_iota(jnp.int32, (1, LANES), 1)
    half = HEAD_DIM // 2
    groups = LANES // half
    packed = tm // groups
    group = lane // half
    pos = pos_ref[...].astype(F32)
    pos_packed = jnp.broadcast_to(pos[0:packed], (packed, LANES))
    for j in range(1, groups):
        pos_packed = jnp.where(group == j, pos[j * packed:(j + 1) * packed], pos_packed)
    ang = pos_packed * invf_ref[...]
    for t, table in enumerate((jnp.cos(ang), jnp.sin(ang))):
        for j in range(groups):
            x = jnp.where(group == j, table, 0.0)
            shift = LANES // 2
            while shift >= half:
                x = x + pltpu.roll(x, shift, axis=1)
                shift //= 2
            stage_ref[0, N_SLABS + t, j * packed:(j + 1) * packed, :] = x

    first_half = (lane % HEAD_DIM) < (HEAD_DIM // 2)
    scale = HEAD_DIM ** -0.5 * LOG2_E

    prepared = []
    for g, dil in enumerate(DILATIONS):
        rows = tm // dil
        src = stage_ref.at[(g - 1) % 2] if g else stage_ref.at[0]
        if g == 0:
            slabs = [src[s] for s in range(N_SLABS + 2)]
        else:
            inner = DILATIONS[g - 1]
            ratio = dil // inner
            starts = [(r % inner) * (tm // inner) + r // inner for r in range(dil)]
            slabs = [jnp.concatenate([src[s, pl.ds(st, rows, stride=ratio), :] for st in starts], axis=0)
                     for s in range(N_SLABS + 2)]
            if g < N_BRANCHES - 1:
                for s, slab in enumerate(slabs):
                    stage_ref[g % 2, s] = slab
        y = _rms_unit(jnp.concatenate(slabs[:N_SLABS], axis=1))
        cos, sin = slabs[N_SLABS:]
        prepared.append(((y * gq_ref[...]).astype(BF16), (y * gkv_ref[...]).astype(BF16),
                         cos, jnp.where(first_half, -sin, sin)))

    for g, dil in enumerate(DILATIONS):
        rows = tm // dil
        a_q, a_kv, cos, sin_signed = prepared[g]

        def rope(t):
            partner = jnp.where(first_half,
                                pltpu.roll(t, LANES - HEAD_DIM // 2, axis=1),
                                pltpu.roll(t, HEAD_DIM // 2, axis=1))
            return t * cos + partner * sin_signed

        def emit(ref, cols, val):
            if dil == 1:
                ref[:, cols] = val.astype(BF16)
            else:
                for r in range(dil):
                    ref[r, :, cols] = val[r * rows:(r + 1) * rows].astype(BF16)

        q_ref, k_ref, v_ref = out_refs[3 * g:3 * g + 3]
        base = g * D_MODEL
        for c in range(0, D_MODEL, MXU_COLS):
            tq = _dot(a_q, wq_ref[:, base + c:base + c + MXU_COLS])
            tk = _dot(a_kv, wkv_ref[:, base + c:base + c + MXU_COLS])
            tv = _dot(a_kv, wkv_ref[:, Q_WIDTH + base + c:Q_WIDTH + base + c + MXU_COLS])
            for s in range(0, MXU_COLS, LANES):
                cols = slice(c + s, c + s + LANES)
                emit(q_ref, cols, rope(tq[:, s:s + LANES]) * scale)
                emit(k_ref, cols, rope(tk[:, s:s + LANES]))
            emit(v_ref, slice(c, c + MXU_COLS), tv)


def _qkv(h, pos, inv_freq, g_kv, g_q, w_kv, w_q):
    tokens, d = h.shape
    tm = QKV_TILE
    assert CHUNK % tm == 0 and tokens % CHUNK == 0
    tiles_per_chunk = CHUNK // tm
    n_chunks = tokens // CHUNK
    row = pl.BlockSpec((tm, d), lambda i: (i, 0))
    out_specs, out_shapes = [], []
    for dil in DILATIONS:
        if dil == 1:
            spec = row
            shape = jax.ShapeDtypeStruct((tokens, d), BF16)
        else:
            assert (tm // dil) % BF16_SUBLANES == 0
            spec = pl.BlockSpec((None, dil, tm // dil, d),
                                lambda i: (i // tiles_per_chunk, 0, i % tiles_per_chunk, 0))
            shape = jax.ShapeDtypeStruct((n_chunks, dil, CHUNK // dil, d), BF16)
        out_specs += [spec] * 3
        out_shapes += [shape] * 3
    outs = pl.pallas_call(
        _qkv_kernel,
        grid=(tokens // tm,),
        in_specs=[row, pl.BlockSpec((tm, 1), lambda i: (i, 0)), _resident((1, LANES)),
                  _resident((1, d)), _resident((1, d)), _resident(w_kv.shape), _resident(w_q.shape)],
        out_specs=out_specs,
        out_shape=out_shapes,
        scratch_shapes=[pltpu.VMEM((2, N_SLABS + 2, tm, LANES), F32)],
        compiler_params=_params("arbitrary"),
        name="qkv_rope",
    )(h, pos, inv_freq, g_kv, g_q, w_kv, w_q)
    dils = [dil for dil in DILATIONS for _ in range(3)]
    return [o.reshape(n_chunks, dil, CHUNK // dil, d) for o, dil in zip(outs, dils)]


ATTN_OPERANDS = 5


def _attn_kernel(*refs):
    n_in = ATTN_OPERANDS * N_BRANCHES
    in_refs, o_ref, scratch = refs[:n_in], refs[n_in], refs[n_in + 1:]
    cap_ref = scratch[-1]
    chunk = pl.program_id(2)

    lane = lax.broadcasted_iota(jnp.int32, (1, LANES), 1)
    low_head = lane < HEAD_DIM
    qi = lax.broadcasted_iota(jnp.int32, (BAND, 2 * BAND), 0)
    kj = lax.broadcasted_iota(jnp.int32, (BAND, 2 * BAND), 1)
    dist = qi + BAND - kj
    in_band = (dist >= 0) & (dist <= BAND)
    first_mask = in_band & ((kj >= BAND) | (chunk > 0))
    cap_ref[0] = jnp.where(in_band, F32_MAX, NEG_INF)
    cap_ref[1] = jnp.where(first_mask, F32_MAX, NEG_INF)
    ones = jnp.ones((2 * BAND, LANES), BF16)
    nt = (((1,), (1,)), ((), ()))

    def block(lanes, g, r, n):
        q_ref, k_ref, v_ref, ktail_ref, vtail_ref = in_refs[ATTN_OPERANDS * g:ATTN_OPERANDS * (g + 1)]
        q2 = q_ref[r, n * BAND:(n + 1) * BAND, lanes]
        if n == 0:
            k2 = jnp.concatenate([ktail_ref[r, :, lanes], k_ref[r, 0:BAND, lanes]], axis=0)
            v2 = jnp.concatenate([vtail_ref[r, :, lanes], v_ref[r, 0:BAND, lanes]], axis=0)
        else:
            k2 = k_ref[r, (n - 1) * BAND:(n + 1) * BAND, lanes]
            v2 = v_ref[r, (n - 1) * BAND:(n + 1) * BAND, lanes]
        zero = jnp.zeros_like(q2)
        qs = jnp.concatenate([jnp.where(low_head, q2, zero), jnp.where(low_head, zero, q2)], axis=0)
        s = lax.dot_general(qs, k2, nt, preferred_element_type=F32)
        cap = cap_ref[1 if n == 0 else 0]
        s = jnp.minimum(s, jnp.concatenate([cap, cap], axis=0))
        m = jnp.max(s, axis=-1, keepdims=True)
        p = jnp.exp2(s - m).astype(BF16)
        out = _dot(p, jnp.concatenate([v2, ones], axis=1))
        acc = jnp.where(low_head, out[0:BAND, 0:LANES], out[BAND:, 0:LANES])
        l = jnp.where(low_head, out[0:BAND, LANES:], out[BAND:, LANES:])
        m2 = jnp.where(low_head, m[0:BAND], m[BAND:])
        return m2, l, acc

    def merge(a, b):
        top = jnp.maximum(a[0], b[0])
        ea = jnp.exp2(a[0] - top)
        eb = jnp.exp2(b[0] - top)
        return top, ea * a[1] + eb * b[1], ea * a[2] + eb * b[2]

    for pair in range(PAIRS_PER_STEP):
        lanes = slice(pair * LANES, (pair + 1) * LANES)
        stages = (scratch[6 * pair:6 * pair + 3], scratch[6 * pair + 3:6 * pair + 6])
        for g in range(N_BRANCHES - 1, -1, -1):
            dil = DILATIONS[g]
            per_run = CHUNK // dil // BAND
            for idx in range(BLOCKS_PER_CHUNK):
                r, n = divmod(idx, per_run)
                triple = block(lanes, g, r, n)
                own_rows = slice(idx * BAND, (idx + 1) * BAND)
                if g < N_BRANCHES - 1:
                    triple = merge(triple, [ref[own_rows, :] for ref in stages[(g + 1) % 2]])
                if g > 0:
                    inner = DILATIONS[g - 1]
                    ratio = dil // inner
                    start = (r % inner) * (CHUNK // inner) + r // inner + n * BAND * ratio
                    for ref, val in zip(stages[g % 2], triple):
                        ref[pl.ds(start, BAND, stride=ratio), :] = val
                else:
                    o_ref[own_rows, lanes] = (triple[2] / triple[1]).astype(o_ref.dtype)


def _attention(qkv, batch, seq, casts):
    chunks_per_seq = seq // CHUNK

    def here(b, j, c):
        return (b * chunks_per_seq + c, 0, 0, j)

    def tail_of_previous(b, j, c, last_block):
        return (b * chunks_per_seq + jnp.maximum(c - 1, 0), 0, last_block, j)

    in_specs, operands = [], []
    for g, dil in enumerate(DILATIONS):
        run = CHUNK // dil
        cur = pl.BlockSpec((None, dil, run, PAIRS_PER_STEP * LANES), here)
        tail = pl.BlockSpec((None, dil, BAND, PAIRS_PER_STEP * LANES),
                            lambda b, j, c, last=run // BAND - 1: tail_of_previous(b, j, c, last))
        q, k, v = qkv[3 * g:3 * g + 3]
        in_specs += [cur, cur, cur, tail, tail]
        operands += [q, k, v, k, v]
    scratch = [pltpu.VMEM((CHUNK, LANES), F32)] * (6 * PAIRS_PER_STEP)
    scratch += [pltpu.VMEM((2, BAND, 2 * BAND), F32)]
    groups = N_HEAD_PAIRS // PAIRS_PER_STEP
    steps = batch * groups * chunks_per_seq
    c_in, c_out, c_shape = _cast_plan(
        casts, steps, lambda b, j, c: (b * groups + j) * chunks_per_seq + c)
    n_in = len(in_specs)
    outs = pl.pallas_call(
        _with_casts(_attn_kernel, n_in, 1, len(casts)),
        grid=(batch, groups, chunks_per_seq),
        in_specs=in_specs + c_in,
        out_specs=[pl.BlockSpec((CHUNK, PAIRS_PER_STEP * LANES),
                                lambda b, j, c: (b * chunks_per_seq + c, j))] + c_out,
        out_shape=[jax.ShapeDtypeStruct((batch * seq, D_MODEL), BF16)] + c_shape,
        scratch_shapes=scratch,
        compiler_params=_params("arbitrary", "arbitrary", "arbitrary"),
        name="dilated_attn",
    )(*operands, *[w for w, _ in casts])
    return outs[0], outs[1:]


def kernel(x, positions, mix_norm_pre, mix_norm_post, ffn_norm_pre, ffn_norm_post, ffn_w_gate_up,
           ffn_w_down, conv_w_in, conv_w, conv_w_out, kv_norm, w_kv, w_q, w_o):
    batch, seq, d = x.shape
    tokens = batch * seq
    for window, dilation in BRANCHES:
        assert window // dilation == BAND and seq % (BAND * dilation) == 0

    def gain(g):
        return g.reshape(1, d).astype(F32)

    half = HEAD_DIM // 2
    inv_freq = ROPE_THETA ** (-jnp.arange(half, dtype=F32) / half)
    inv_freq = jnp.tile(inv_freq, LANES // half).reshape(1, LANES)
    pos = positions.reshape(tokens, 1)

    h, (w_gu0, w_dn0) = _conv_mixer(
        x, gain(mix_norm_pre[0]), conv_w_in[0].astype(BF16), conv_w[0], conv_w_out[0].astype(BF16),
        gain(mix_norm_post[0]), casts=[(ffn_w_gate_up, 0), (ffn_w_down, 0)])
    h = h.reshape(tokens, d)
    h, (w_kv16, w_q16) = _ffn(h, gain(ffn_norm_pre[0]), w_gu0, w_dn0, gain(ffn_norm_post[0]),
                              casts=[(w_kv[None], 0), (w_q, 0)])

    qkv = _qkv(h, pos, inv_freq, gain(kv_norm), gain(mix_norm_pre[1]), w_kv16, w_q16)
    a, (w_gu1, w_dn1, w_o16) = _attention(
        qkv, batch, seq, casts=[(ffn_w_gate_up, 1), (ffn_w_down, 1), (w_o, 0)])
    h, _ = _ffn(h, gain(ffn_norm_pre[1]), w_gu1, w_dn1, gain(ffn_norm_post[1]),
                attn=(a, w_o16, gain(mix_norm_post[1])))
    return h.reshape(batch, seq, d)
```

```python
import math

import jax
import jax.numpy as jnp
from jax import lax
from jax.experimental import pallas as pl
from jax.experimental.pallas import tpu as pltpu

D_MODEL = 1024
HEAD_DIM = 64
N_HEADS = D_MODEL // HEAD_DIM
BRANCHES = ((128, 1), (512, 4), (2048, 16))
DILATIONS = tuple(d for _, d in BRANCHES)
N_BRANCHES = len(BRANCHES)
Q_WIDTH = N_BRANCHES * N_HEADS * HEAD_DIM
CONV_WIDTH = 3
ROPE_THETA = 10000.0
RMS_EPS = 1e-6
NEG_INF = -1e30
LOG2_E = math.log2(math.e)
F32_MAX = 3.0e38

LANES = 128
BF16_SUBLANES = 16
MXU_COLS = 256
BAND = 128
CHUNK = BAND * max(DILATIONS)
BLOCKS_PER_CHUNK = CHUNK // BAND
HEADS_PER_VREG = LANES // HEAD_DIM
N_HEAD_PAIRS = N_HEADS // HEADS_PER_VREG
N_SLABS = D_MODEL // LANES
VMEM_LIMIT_BYTES = 56 * 1024 * 1024

TOKEN_TILE = 1024
FFN_TILE = 512
QKV_TILE = 512
ROW_SPANS = 4
PAIRS_PER_STEP = 2
CARRY_ROWS = 8

F32 = jnp.float32
BF16 = jnp.bfloat16


def _dot(a, b):
    return jnp.dot(a, b, preferred_element_type=F32)


def _rms_unit(x):
    return x * lax.rsqrt(jnp.mean(x * x, axis=-1, keepdims=True) + RMS_EPS)


def _resident(shape):
    return pl.BlockSpec(shape, lambda *_: (0,) * len(shape), pipeline_mode=pl.Buffered(1))


def _params(*semantics):
    return pltpu.CompilerParams(dimension_semantics=semantics, vmem_limit_bytes=VMEM_LIMIT_BYTES)


def _cast_plan(weights, steps, linear_step):
    in_specs, out_specs, out_shapes = [], [], []
    for w, layer in weights:
        _, rows, cols = w.shape
        block = -(-rows // steps)
        block = -(-block // BF16_SUBLANES) * BF16_SUBLANES
        while rows % block:
            block += BF16_SUBLANES
        last = rows // block - 1

        def src(*idx, layer=layer, last=last):
            return (layer, jnp.minimum(linear_step(*idx), last), 0)

        def dst(*idx, last=last):
            return (jnp.minimum(linear_step(*idx), last), 0)

        in_specs.append(pl.BlockSpec((None, block, cols), src))
        out_specs.append(pl.BlockSpec((block, cols), dst))
        out_shapes.append(jax.ShapeDtypeStruct((rows, cols), BF16))
    return in_specs, out_specs, out_shapes


def _with_casts(body, n_in, n_out, n_cast):
    def kernel(*refs):
        ins, refs = refs[:n_in], refs[n_in:]
        cast_src, refs = refs[:n_cast], refs[n_cast:]
        outs, refs = refs[:n_out], refs[n_out:]
        cast_dst, scratch = refs[:n_cast], refs[n_cast:]
        body(*ins, *outs, *scratch)
        for s_ref, d_ref in zip(cast_src, cast_dst):
            d_ref[...] = s_ref[...].astype(BF16)
    return kernel


def _conv_mixer_kernel(x_ref, gpre_ref, win_ref, cw_ref, wout_ref, gpost_ref, o_ref, ubuf_ref):
    j = pl.program_id(1)
    tm = x_ref.shape[0]

    @pl.when(j == 0)
    def _():
        ubuf_ref[0:CARRY_ROWS, :] = jnp.zeros((CARRY_ROWS, D_MODEL), F32)

    @pl.when(j > 0)
    def _():
        ubuf_ref[0:CARRY_ROWS, :] = ubuf_ref[tm:tm + CARRY_ROWS, :]

    span = tm // ROW_SPANS
    xs = [x_ref[k * span:(k + 1) * span, :] for k in range(ROW_SPANS)]
    hns = [(_rms_unit(x) * gpre_ref[...]).astype(BF16) for x in xs]
    gates = [_dot(hn, win_ref[:, 0:D_MODEL]) for hn in hns]
    us = [_dot(hn, win_ref[:, D_MODEL:2 * D_MODEL]) * _dot(hn, win_ref[:, 2 * D_MODEL:3 * D_MODEL])
          for hn in hns]
    ys = []
    for k, (gate, u) in enumerate(zip(gates, us)):
        base = CARRY_ROWS + k * span
        ubuf_ref[base:base + span, :] = u
        u1 = ubuf_ref[base - 1:base - 1 + span, :]
        u2 = ubuf_ref[base - 2:base - 2 + span, :]
        conv = cw_ref[0:1, :] * u2 + cw_ref[1:2, :] * u1 + cw_ref[2:3, :] * u
        ys.append(_dot((gate * conv).astype(BF16), wout_ref[...]))
    for k, (x, y) in enumerate(zip(xs, ys)):
        o_ref[k * span:(k + 1) * span, :] = x + _rms_unit(y) * gpost_ref[...]


def _conv_mixer(x, g_pre, w_in, conv_w, w_out, g_post, casts):
    batch, seq, d = x.shape
    tm = TOKEN_TILE
    tiles = seq // tm
    row = pl.BlockSpec((None, tm, d), lambda b, j: (b, j, 0))
    c_in, c_out, c_shape = _cast_plan(casts, batch * tiles, lambda b, j: b * tiles + j)
    in_specs = [row, _resident((1, d)), _resident(w_in.shape), _resident(conv_w.shape),
                _resident(w_out.shape), _resident((1, d))]
    outs = pl.pallas_call(
        _with_casts(_conv_mixer_kernel, len(in_specs), 1, len(casts)),
        grid=(batch, tiles),
        in_specs=in_specs + c_in,
        out_specs=[row] + c_out,
        out_shape=[jax.ShapeDtypeStruct(x.shape, F32)] + c_shape,
        scratch_shapes=[pltpu.VMEM((tm + CARRY_ROWS, d), F32)],
        compiler_params=_params("arbitrary", "arbitrary"),
        name="conv_mixer",
    )(x, g_pre, w_in, conv_w, w_out, g_post, *[w for w, _ in casts])
    return outs[0], outs[1:]


def _swiglu_residual(hs, gpre_ref, wgu_ref, wd_ref, gpost_ref, act_ref):
    d_ff = wd_ref.shape[0]
    span = hs[0].shape[0]
    hns = [(_rms_unit(h) * gpre_ref[...]).astype(BF16) for h in hs]
    for c in range(0, d_ff, MXU_COLS):
        for k, hn in enumerate(hns):
            g = _dot(hn, wgu_ref[:, c:c + MXU_COLS])
            u = _dot(hn, wgu_ref[:, d_ff + c:d_ff + c + MXU_COLS])
            act_ref[k * span:(k + 1) * span, c:c + MXU_COLS] = (g * jax.nn.sigmoid(g) * u).astype(BF16)
    fs = [_dot(act_ref[k * span:(k + 1) * span, :], wd_ref[...]) for k in range(len(hs))]
    return [h + _rms_unit(f) * gpost_ref[...] for h, f in zip(hs, fs)]


def _row_spans(ref):
    span = ref.shape[0] // ROW_SPANS
    return [ref[k * span:(k + 1) * span, :] for k in range(ROW_SPANS)]


def _ffn_kernel(h_ref, gpre_ref, wgu_ref, wd_ref, gpost_ref, o_ref, act_ref):
    outs = _swiglu_residual(_row_spans(h_ref), gpre_ref, wgu_ref, wd_ref, gpost_ref, act_ref)
    o_ref[...] = jnp.concatenate(outs, axis=0)


def _attn_out_ffn_kernel(h_ref, a_ref, wo_ref, gmix_ref, gpre_ref, wgu_ref, wd_ref, gpost_ref,
                         o_ref, act_ref):
    hs = [h + _rms_unit(_dot(a, wo_ref[...])) * gmix_ref[...]
          for h, a in zip(_row_spans(h_ref), _row_spans(a_ref))]
    outs = _swiglu_residual(hs, gpre_ref, wgu_ref, wd_ref, gpost_ref, act_ref)
    o_ref[...] = jnp.concatenate(outs, axis=0)


def _ffn(h, g_pre, w_gate_up, w_down, g_post, attn=None, casts=()):
    tokens, d = h.shape
    d_ff = w_down.shape[0]
    assert d_ff % MXU_COLS == 0
    tm = FFN_TILE if attn is None else TOKEN_TILE
    row = pl.BlockSpec((tm, d), lambda i: (i, 0))
    ffn_specs =[_resident((1, d)), _resident(w_gate_up.shape), _resident(w_down.shape),
                 _resident((1, d))]
    if attn is None:
        body, specs, args = _ffn_kernel, [row], (h,)
    else:
        a, w_o, g_mix = attn
        body = _attn_out_ffn_kernel
        specs = [row, row, _resident(w_o.shape), _resident((1, d))]
        args = (h, a, w_o, g_mix)
    in_specs = specs + ffn_specs
    c_in, c_out, c_shape = _cast_plan(casts, tokens // tm, lambda i: i)
    outs = pl.pallas_call(
        _with_casts(body, len(in_specs), 1, len(casts)),
        grid=(tokens // tm,),
        in_specs=in_specs + c_in,
        out_specs=[row] + c_out,
        out_shape=[jax.ShapeDtypeStruct(h.shape, F32)] + c_shape,
        scratch_shapes=[pltpu.VMEM((tm, d_ff), BF16)],
        compiler_params=_params("arbitrary"),
        name="ffn" if attn is None else "attn_out_ffn",
    )(*args, g_pre, w_gate_up, w_down, g_post, *[w for w, _ in casts])
    return outs[0], outs[1:]


def _qkv_kernel(h_ref, pos_ref, invf_ref, gkv_ref, gq_ref, wkv_ref, wq_ref, *rest):
    out_refs, stage_ref = rest[:3 * N_BRANCHES], rest[3 * N_BRANCHES]
    tm = h_ref.shape[0]

    for s in range(N_SLABS):
        stage_ref[0, s] = h_ref[:, s * LANES:(s + 1) * LANES]

    lane = lax.broadcasted_iota(jnp.int32, (1, LANES), 1)
    half = HEAD_DIM // 2
    groups = LANES // half
    packed = tm // groups
    group = lane // half
    pos = pos_ref[...].astype(F32)
    pos_packed = jnp.broadcast_to(pos[0:packed], (packed, LANES))
    for j in range(1, groups):
        pos_packed = jnp.where(group == j, pos[j * packed:(j + 1) * packed], pos_packed)
    ang = pos_packed * invf_ref[...]
    for t, table in enumerate((jnp.cos(ang), jnp.sin(ang))):
        for j in range(groups):
            x = jnp.where(group == j, table, 0.0)
            shift = LANES // 2
            while shift >= half:
                x = x + pltpu.roll(x, shift, axis=1)
                shift //= 2
            stage_ref[0, N_SLABS + t, j * packed:(j + 1) * packed, :] = x

    first_half = (lane % HEAD_DIM) < (HEAD_DIM // 2)
    scale = HEAD_DIM ** -0.5 * LOG2_E

    prepared = []
    for g, dil in enumerate(DILATIONS):
        rows = tm // dil
        src = stage_ref.at[(g - 1) % 2] if g else stage_ref.at[0]
        if g == 0:
            slabs = [src[s] for s in range(N_SLABS + 2)]
        else:
            inner = DILATIONS[g - 1]
            ratio = dil // inner
            starts = [(r % inner) * (tm // inner) + r // inner for r in range(dil)]
            slabs = [jnp.concatenate([src[s, pl.ds(st, rows, stride=ratio), :] for st in starts], axis=0)
                     for s in range(N_SLABS + 2)]
            if g < N_BRANCHES - 1:
                for s, slab in enumerate(slabs):
                    stage_ref[g % 2, s] = slab
        y = _rms_unit(jnp.concatenate(slabs[:N_SLABS], axis=1))
        cos, sin = slabs[N_SLABS:]
        prepared.append(((y * gq_ref[...]).astype(BF16), (y * gkv_ref[...]).astype(BF16),
                         cos, jnp.where(first_half, -sin, sin)))

    for g, dil in enumerate(DILATIONS):
        rows = tm // dil
        a_q, a_kv, cos, sin_signed = prepared[g]

        def rope(t):
            partner = jnp.where(first_half,
                                pltpu.roll(t, LANES - HEAD_DIM // 2, axis=1),
                                pltpu.roll(t, HEAD_DIM // 2, axis=1))
            return t * cos + partner * sin_signed

        def emit(ref, cols, val):
            if dil == 1:
                ref[:, cols] = val.astype(BF16)
            else:
                for r in range(dil):
                    ref[r, :, cols] = val[r * rows:(r + 1) * rows].astype(BF16)

        q_ref, k_ref, v_ref = out_refs[3 * g:3 * g + 3]
        base = g * D_MODEL
        for c in range(0, D_MODEL, MXU_COLS):
            tq = _dot(a_q, wq_ref[:, base + c:base + c + MXU_COLS])
            tk = _dot(a_kv, wkv_ref[:, base + c:base + c + MXU_COLS])
            tv = _dot(a_kv, wkv_ref[:, Q_WIDTH + base + c:Q_WIDTH + base + c + MXU_COLS])
            for s in range(0, MXU_COLS, LANES):
                cols = slice(c + s, c + s + LANES)
                emit(q_ref, cols, rope(tq[:, s:s + LANES]) * scale)
                emit(k_ref, cols, rope(tk[:, s:s + LANES]))
            emit(v_ref, slice(c, c + MXU_COLS), tv)


def _qkv(h, pos, inv_freq, g_kv, g_q, w_kv, w_q):
    tokens, d = h.shape
    tm = QKV_TILE
    assert CHUNK % tm == 0 and tokens % CHUNK == 0
    tiles_per_chunk = CHUNK // tm
    n_chunks = tokens // CHUNK
    row = pl.BlockSpec((tm, d), lambda i: (i, 0))
    out_specs, out_shapes = [], []
    for dil in DILATIONS:
        if dil == 1:
            spec = row
            shape = jax.ShapeDtypeStruct((tokens, d), BF16)
        else:
            assert (tm // dil) % BF16_SUBLANES == 0
            spec = pl.BlockSpec((None, dil, tm // dil, d),
                                lambda i: (i // tiles_per_chunk, 0, i % tiles_per_chunk, 0))
            shape = jax.ShapeDtypeStruct((n_chunks, dil, CHUNK // dil, d), BF16)
        out_specs += [spec] * 3
        out_shapes += [shape] * 3
    outs = pl.pallas_call(
        _qkv_kernel,
        grid=(tokens // tm,),
        in_specs=[row, pl.BlockSpec((tm, 1), lambda i: (i, 0)), _resident((1, LANES)),
                  _resident((1, d)), _resident((1, d)), _resident(w_kv.shape), _resident(w_q.shape)],
        out_specs=out_specs,
        out_shape=out_shapes,
        scratch_shapes=[pltpu.VMEM((2, N_SLABS + 2, tm, LANES), F32)],
        compiler_params=_params("arbitrary"),
        name="qkv_rope",
    )(h, pos, inv_freq, g_kv, g_q, w_kv, w_q)
    dils = [dil for dil in DILATIONS for _ in range(3)]
    return [o.reshape(n_chunks, dil, CHUNK // dil, d) for o, dil in zip(outs, dils)]


ATTN_OPERANDS = 5


def _attn_kernel(*refs):
    n_in = ATTN_OPERANDS * N_BRANCHES
    in_refs, o_ref, scratch = refs[:n_in], refs[n_in], refs[n_in + 1:]
    cap_ref = scratch[-1]
    chunk = pl.program_id(2)

    lane = lax.broadcasted_iota(jnp.int32, (1, LANES), 1)
    low_head = lane < HEAD_DIM
    qi = lax.broadcasted_iota(jnp.int32, (BAND, 2 * BAND), 0)
    kj = lax.broadcasted_iota(jnp.int32, (BAND, 2 * BAND), 1)
    dist = qi + BAND - kj
    in_band = (dist >= 0) & (dist <= BAND)
    first_mask = in_band & ((kj >= BAND) | (chunk > 0))
    cap_ref[0] = jnp.where(in_band, F32_MAX, NEG_INF)
    cap_ref[1] = jnp.where(first_mask, F32_MAX, NEG_INF)
    ones = jnp.ones((2 * BAND, LANES), BF16)
    nt = (((1,), (1,)), ((), ()))

    def block(lanes, g, r, n):
        q_ref, k_ref, v_ref, ktail_ref, vtail_ref = in_refs[ATTN_OPERANDS * g:ATTN_OPERANDS * (g + 1)]
        q2 = q_ref[r, n * BAND:(n + 1) * BAND, lanes]
        if n == 0:
            k2 = jnp.concatenate([ktail_ref[r, :, lanes], k_ref[r, 0:BAND, lanes]], axis=0)
            v2 = jnp.concatenate([vtail_ref[r, :, lanes], v_ref[r, 0:BAND, lanes]], axis=0)
        else:
            k2 = k_ref[r, (n - 1) * BAND:(n + 1) * BAND, lanes]
            v2 = v_ref[r, (n - 1) * BAND:(n + 1) * BAND, lanes]
        zero = jnp.zeros_like(q2)
        qs = jnp.concatenate([jnp.where(low_head, q2, zero), jnp.where(low_head, zero, q2)], axis=0)
        s = lax.dot_general(qs, k2, nt, preferred_element_type=F32)
        cap = cap_ref[1 if n == 0 else 0]
        s = jnp.minimum(s, jnp.concatenate([cap, cap], axis=0))
        m = jnp.max(s, axis=-1, keepdims=True)
        p = jnp.exp2(s - m).astype(BF16)
        out = _dot(p, jnp.concatenate([v2, ones], axis=1))
        acc = jnp.where(low_head, out[0:BAND, 0:LANES], out[BAND:, 0:LANES])
        l = jnp.where(low_head, out[0:BAND, LANES:], out[BAND:, LANES:])
        m2 = jnp.where(low_head, m[0:BAND], m[BAND:])
        return m2, l, acc

    def merge(a, b):
        top = jnp.maximum(a[0], b[0])
        ea = jnp.exp2(a[0] - top)
        eb = jnp.exp2(b[0] - top)
        return top, ea * a[1] + eb * b[1], ea * a[2] + eb * b[2]

    for pair in range(PAIRS_PER_STEP):
        lanes = slice(pair * LANES, (pair + 1) * LANES)
        stages = (scratch[6 * pair:6 * pair + 3], scratch[6 * pair + 3:6 * pair + 6])
        for g in range(N_BRANCHES - 1, -1, -1):
            dil = DILATIONS[g]
            per_run = CHUNK // dil // BAND
            for idx in range(BLOCKS_PER_CHUNK):
                r, n = divmod(idx, per_run)
                triple = block(lanes, g, r, n)
                own_rows = slice(idx * BAND, (idx + 1) * BAND)
                if g < N_BRANCHES - 1:
                    triple = merge(triple, [ref[own_rows, :] for ref in stages[(g + 1) % 2]])
                if g > 0:
                    inner = DILATIONS[g - 1]
                    ratio = dil // inner
                    start = (r % inner) * (CHUNK // inner) + r // inner + n * BAND * ratio
                    for ref, val in zip(stages[g % 2], triple):
                        ref[pl.ds(start, BAND, stride=ratio), :] = val
                else:
                    o_ref[own_rows, lanes] = (triple[2] / triple[1]).astype(o_ref.dtype)


def _attention(qkv, batch, seq, casts):
    chunks_per_seq = seq // CHUNK

    def here(b, j, c):
        return (b * chunks_per_seq + c, 0, 0, j)

    def tail_of_previous(b, j, c, last_block):
        return (b * chunks_per_seq + jnp.maximum(c - 1, 0), 0, last_block, j)

    in_specs, operands = [], []
    for g, dil in enumerate(DILATIONS):
        run = CHUNK // dil
        cur = pl.BlockSpec((None, dil, run, PAIRS_PER_STEP * LANES), here)
        tail = pl.BlockSpec((None, dil, BAND, PAIRS_PER_STEP * LANES),
                            lambda b, j, c, last=run // BAND - 1: tail_of_previous(b, j, c, last))
        q, k, v = qkv[3 * g:3 * g + 3]
        in_specs += [cur, cur, cur, tail, tail]
        operands += [q, k, v, k, v]
    scratch = [pltpu.VMEM((CHUNK, LANES), F32)] * (6 * PAIRS_PER_STEP)
    scratch += [pltpu.VMEM((2, BAND, 2 * BAND), F32)]
    groups = N_HEAD_PAIRS // PAIRS_PER_STEP
    steps = batch * groups * chunks_per_seq
    c_in, c_out, c_shape = _cast_plan(
        casts, steps, lambda b, j, c: (b * groups + j) * chunks_per_seq + c)
    n_in = len(in_specs)
    outs = pl.pallas_call(
        _with_casts(_attn_kernel, n_in, 1, len(casts)),
        grid=(batch, groups, chunks_per_seq),
        in_specs=in_specs + c_in,
        out_specs=[pl.BlockSpec((CHUNK, PAIRS_PER_STEP * LANES),
                                lambda b, j, c: (b * chunks_per_seq + c, j))] + c_out,
        out_shape=[jax.ShapeDtypeStruct((batch * seq, D_MODEL), BF16)] + c_shape,
        scratch_shapes=scratch,
        compiler_params=_params("arbitrary", "arbitrary", "arbitrary"),
        name="dilated_attn",
    )(*operands, *[w for w, _ in casts])
    return outs[0], outs[1:]


def kernel(x, positions, mix_norm_pre, mix_norm_post, ffn_norm_pre, ffn_norm_post, ffn_w_gate_up,
           ffn_w_down, conv_w_in, conv_w, conv_w_out, kv_norm, w_kv, w_q, w_o):
    batch, seq, d = x.shape
    tokens = batch * seq
    for window, dilation in BRANCHES:
        assert window // dilation == BAND and seq % (BAND * dilation) == 0

    def gain(g):
        return g.reshape(1, d).astype(F32)

    half = HEAD_DIM // 2
    inv_freq = ROPE_THETA ** (-jnp.arange(half, dtype=F32) / half)
    inv_freq = jnp.tile(inv_freq, LANES // half).reshape(1, LANES)
    pos = positions.reshape(tokens, 1)

    h, (w_gu0, w_dn0) = _conv_mixer(
        x, gain(mix_norm_pre[0]), conv_w_in[0].astype(BF16), conv_w[0], conv_w_out[0].astype(BF16),
        gain(mix_norm_post[0]), casts=[(ffn_w_gate_up, 0), (ffn_w_down, 0)])
    h = h.reshape(tokens, d)
    h, (w_kv16, w_q16) = _ffn(h, gain(ffn_norm_pre[0]), w_gu0, w_dn0, gain(ffn_norm_post[0]),
                              casts=[(w_kv[None], 0), (w_q, 0)])

    qkv = _qkv(h, pos, inv_freq, gain(kv_norm), gain(mix_norm_pre[1]), w_kv16, w_q16)
    a, (w_gu1, w_dn1, w_o16) = _attention(
        qkv, batch, seq, casts=[(ffn_w_gate_up, 1), (ffn_w_down, 1), (w_o, 0)])
    h, _ = _ffn(h, gain(ffn_norm_pre[1]), w_gu1, w_dn1, gain(ffn_norm_post[1]),
                attn=(a, w_o16, gain(mix_norm_post[1])))
    return h.reshape(batch, seq, d)
```

```python
import math

import jax
import jax.numpy as jnp
from jax import lax
from jax.experimental import pallas as pl
from jax.experimental.pallas import tpu as pltpu

D_MODEL = 1024
HEAD_DIM = 64
N_HEADS = D_MODEL // HEAD_DIM
BRANCHES = ((128, 1), (512, 4), (2048, 16))
DILATIONS = tuple(d for _, d in BRANCHES)
N_BRANCHES = len(BRANCHES)
Q_WIDTH = N_BRANCHES * N_HEADS * HEAD_DIM
CONV_WIDTH = 3
ROPE_THETA = 10000.0
RMS_EPS = 1e-6
NEG_INF = -1e30
LOG2_E = math.log2(math.e)
F32_MAX = 3.0e38

LANES = 128
BF16_SUBLANES = 16
MXU_COLS = 256
BAND = 128
CHUNK = BAND * max(DILATIONS)
BLOCKS_PER_CHUNK = CHUNK // BAND
HEADS_PER_VREG = LANES // HEAD_DIM
N_HEAD_PAIRS = N_HEADS // HEADS_PER_VREG
N_SLABS = D_MODEL // LANES
VMEM_LIMIT_BYTES = 56 * 1024 * 1024

TOKEN_TILE = 1024
FFN_TILE = 512
QKV_TILE = 512
ROW_SPANS = 4
PAIRS_PER_STEP = 2
CARRY_ROWS = 8

F32 = jnp.float32
BF16 = jnp.bfloat16


def _dot(a, b):
    return jnp.dot(a, b, preferred_element_type=F32)


def _rms_unit(x):
    return x * lax.rsqrt(jnp.mean(x * x, axis=-1, keepdims=True) + RMS_EPS)


def _resident(shape):
    return pl.BlockSpec(shape, lambda *_: (0,) * len(shape), pipeline_mode=pl.Buffered(1))


def _params(*semantics):
    return pltpu.CompilerParams(dimension_semantics=semantics, vmem_limit_bytes=VMEM_LIMIT_BYTES)


def _cast_plan(weights, steps, linear_step):
    in_specs, out_specs, out_shapes = [], [], []
    for w, layer in weights:
        _, rows, cols = w.shape
        block = -(-rows // steps)
        block = -(-block // BF16_SUBLANES) * BF16_SUBLANES
        while rows % block:
            block += BF16_SUBLANES
        last = rows // block - 1

        def src(*idx, layer=layer, last=last):
            return (layer, jnp.minimum(linear_step(*idx), last), 0)

        def dst(*idx, last=last):
            return (jnp.minimum(linear_step(*idx), last), 0)

        in_specs.append(pl.BlockSpec((None, block, cols), src))
        out_specs.append(pl.BlockSpec((block, cols), dst))
        out_shapes.append(jax.ShapeDtypeStruct((rows, cols), BF16))
    return in_specs, out_specs, out_shapes


def _with_casts(body, n_in, n_out, n_cast):
    def kernel(*refs):
        ins, refs = refs[:n_in], refs[n_in:]
        cast_src, refs = refs[:n_cast], refs[n_cast:]
        outs, refs = refs[:n_out], refs[n_out:]
        cast_dst, scratch = refs[:n_cast], refs[n_cast:]
        body(*ins, *outs, *scratch)
        for s_ref, d_ref in zip(cast_src, cast_dst):
            d_ref[...] = s_ref[...].astype(BF16)
    return kernel


def _conv_mixer_kernel(x_ref, gpre_ref, win_ref, cw_ref, wout_ref, gpost_ref, o_ref, ubuf_ref):
    j = pl.program_id(1)
    tm = x_ref.shape[0]

    @pl.when(j == 0)
    def _():
        ubuf_ref[0:CARRY_ROWS, :] = jnp.zeros((CARRY_ROWS, D_MODEL), F32)

    @pl.when(j > 0)
    def _():
        ubuf_ref[0:CARRY_ROWS, :] = ubuf_ref[tm:tm + CARRY_ROWS, :]

    span = tm // ROW_SPANS
    xs = [x_ref[k * span:(k + 1) * span, :] for k in range(ROW_SPANS)]
    hns = [(_rms_unit(x) * gpre_ref[...]).astype(BF16) for x in xs]
    gates = [_dot(hn, win_ref[:, 0:D_MODEL]) for hn in hns]
    us = [_dot(hn, win_ref[:, D_MODEL:2 * D_MODEL]) * _dot(hn, win_ref[:, 2 * D_MODEL:3 * D_MODEL])
          for hn in hns]
    ys = []
    for k, (gate, u) in enumerate(zip(gates, us)):
        base = CARRY_ROWS + k * span
        ubuf_ref[base:base + span, :] = u
        u1 = ubuf_ref[base - 1:base - 1 + span, :]
        u2 = ubuf_ref[base - 2:base - 2 + span, :]
        conv = cw_ref[0:1, :] * u2 + cw_ref[1:2, :] * u1 + cw_ref[2:3, :] * u
        ys.append(_dot((gate * conv).astype(BF16), wout_ref[...]))
    for k, (x, y) in enumerate(zip(xs, ys)):
        o_ref[k * span:(k + 1) * span, :] = x + _rms_unit(y) * gpost_ref[...]


def _conv_mixer(x, g_pre, w_in, conv_w, w_out, g_post, casts):
    batch, seq, d = x.shape
    tm = TOKEN_TILE
    tiles = seq // tm
    row = pl.BlockSpec((None, tm, d), lambda b, j: (b, j, 0))
    c_in, c_out, c_shape = _cast_plan(casts, batch * tiles, lambda b, j: b * tiles + j)
    in_specs = [row, _resident((1, d)), _resident(w_in.shape), _resident(conv_w.shape),
                _resident(w_out.shape), _resident((1, d))]
    outs = pl.pallas_call(
        _with_casts(_conv_mixer_kernel, len(in_specs), 1, len(casts)),
        grid=(batch, tiles),
        in_specs=in_specs + c_in,
        out_specs=[row] + c_out,
        out_shape=[jax.ShapeDtypeStruct(x.shape, F32)] + c_shape,
        scratch_shapes=[pltpu.VMEM((tm + CARRY_ROWS, d), F32)],
        compiler_params=_params("arbitrary", "arbitrary"),
        name="conv_mixer",
    )(x, g_pre, w_in, conv_w, w_out, g_post, *[w for w, _ in casts])
    return outs[0], outs[1:]


def _swiglu_residual(hs, gpre_ref, wgu_ref, wd_ref, gpost_ref, act_ref):
    d_ff = wd_ref.shape[0]
    span = hs[0].shape[0]
    hns = [(_rms_unit(h) * gpre_ref[...]).astype(BF16) for h in hs]
    for c in range(0, d_ff, MXU_COLS):
        for k, hn in enumerate(hns):
            g = _dot(hn, wgu_ref[:, c:c + MXU_COLS])
            u = _dot(hn, wgu_ref[:, d_ff + c:d_ff + c + MXU_COLS])
            act_ref[k * span:(k + 1) * span, c:c + MXU_COLS] = (g * jax.nn.sigmoid(g) * u).astype(BF16)
    fs = [_dot(act_ref[k * span:(k + 1) * span, :], wd_ref[...]) for k in range(len(hs))]
    return [h + _rms_unit(f) * gpost_ref[...] for h, f in zip(hs, fs)]


def _row_spans(ref):
    span = ref.shape[0] // ROW_SPANS
    return [ref[k * span:(k + 1) * span, :] for k in range(ROW_SPANS)]


def _ffn_kernel(h_ref, gpre_ref, wgu_ref, wd_ref, gpost_ref, o_ref, act_ref):
    outs = _swiglu_residual(_row_spans(h_ref), gpre_ref, wgu_ref, wd_ref, gpost_ref, act_ref)
    o_ref[...] = jnp.concatenate(outs, axis=0)


def _attn_out_ffn_kernel(h_ref, a_ref, wo_ref, gmix_ref, gpre_ref, wgu_ref, wd_ref, gpost_ref,
                         o_ref, act_ref):
    hs = [h + _rms_unit(_dot(a, wo_ref[...])) * gmix_ref[...]
          for h, a in zip(_row_spans(h_ref), _row_spans(a_ref))]
    outs = _swiglu_residual(hs, gpre_ref, wgu_ref, wd_ref, gpost_ref, act_ref)
    o_ref[...] = jnp.concatenate(outs, axis=0)


def _ffn(h, g_pre, w_gate_up, w_down, g_post, attn=None, casts=()):
    tokens, d = h.shape
    d_ff = w_down.shape[0]
    assert d_ff % MXU_COLS == 0
    tm = FFN_TILE if attn is None else TOKEN_TILE
    row = pl.BlockSpec((tm, d), lambda i: (i, 0))
    ffn_specs = [_resident((1, d)), _resident(w_gate_up.shape), _resident(w_down.shape),
                 _resident((1, d))]
    if attn is None:
        body, specs, args = _ffn_kernel, [row], (h,)
    else:
        a, w_o, g_mix = attn
        body = _attn_out_ffn_kernel
        specs = [row, row, _resident(w_o.shape), _resident((1, d))]
        args = (h, a, w_o, g_mix)
    in_specs = specs + ffn_specs
    c_in, c_out, c_shape = _cast_plan(casts, tokens // tm, lambda i: i)
    outs = pl.pallas_call(
        _with_casts(body, len(in_specs), 1, len(casts)),
        grid=(tokens // tm,),
        in_specs=in_specs + c_in,
        out_specs=[row] + c_out,
        out_shape=[jax.ShapeDtypeStruct(h.shape, F32)] + c_shape,
        scratch_shapes=[pltpu.VMEM((tm, d_ff), BF16)],
        compiler_params=_params("arbitrary"),
        name="ffn" if attn is None else "attn_out_ffn",
    )(*args, g_pre, w_gate_up, w_down, g_post, *[w for w, _ in casts])
    return outs[0], outs[1:]


def _qkv_kernel(h_ref, pos_ref, invf_ref, gkv_ref, gq_ref, wkv_ref, wq_ref, *rest):
    out_refs, stage_ref = rest[:3 * N_BRANCHES], rest[3 * N_BRANCHES]
    tm = h_ref.shape[0]

    for s in range(N_SLABS):
        stage_ref[0, s] = h_ref[:, s * LANES:(s + 1) * LANES]

    lane = lax.broadcasted_iota(jnp.int32, (1, LANES), 1)
    half = HEAD_DIM // 2
    groups = LANES // half
    packed = tm // groups
    group = lane // half
    pos = pos_ref[...].astype(F32)
    pos_packed = jnp.broadcast_to(pos[0:packed], (packed, LANES))
    for j in range(1, groups):
        pos_packed = jnp.where(group == j, pos[j * packed:(j + 1) * packed], pos_packed)
    ang = pos_packed * invf_ref[...]
    for t, table in enumerate((jnp.cos(ang), jnp.sin(ang))):
        for j in range(groups):
            x = jnp.where(group == j, table, 0.0)
            shift = LANES // 2
            while shift >= half:
                x = x + pltpu.roll(x, shift, axis=1)
                shift //= 2
            stage_ref[0, N_SLABS + t, j * packed:(j + 1) * packed, :] = x

    first_half = (lane % HEAD_DIM) < (HEAD_DIM // 2)
    scale = HEAD_DIM ** -0.5 * LOG2_E

    prepared = []
    for g, dil in enumerate(DILATIONS):
        rows = tm // dil
        src = stage_ref.at[(g - 1) % 2] if g else stage_ref.at[0]
        if g == 0:
            slabs = [src[s] for s in range(N_SLABS + 2)]
        else:
            inner = DILATIONS[g - 1]
            ratio = dil // inner
            starts = [(r % inner) * (tm // inner) + r // inner for r in range(dil)]
            slabs = [jnp.concatenate([src[s, pl.ds(st, rows, stride=ratio), :] for st in starts], axis=0)
                     for s in range(N_SLABS + 2)]
            if g < N_BRANCHES - 1:
                for s, slab in enumerate(slabs):
                    stage_ref[g % 2, s] = slab
        y = _rms_unit(jnp.concatenate(slabs[:N_SLABS], axis=1))
        cos, sin = slabs[N_SLABS:]
        prepared.append(((y * gq_ref[...]).astype(BF16), (y * gkv_ref[...]).astype(BF16),
                         cos, jnp.where(first_half, -sin, sin)))

    for g, dil in enumerate(DILATIONS):
        rows = tm // dil
        a_q, a_kv, cos, sin_signed = prepared[g]

        def rope(t):
            partner = jnp.where(first_half,
                                pltpu.roll(t, LANES - HEAD_DIM // 2, axis=1),
                                pltpu.roll(t, HEAD_DIM // 2, axis=1))
            return t * cos + partner * sin_signed

        def emit(ref, cols, val):
            if dil == 1:
                ref[:, cols] = val.astype(BF16)
            else:
                for r in range(dil):
                    ref[r, :, cols] = val[r * rows:(r + 1) * rows].astype(BF16)

        q_ref, k_ref, v_ref = out_refs[3 * g:3 * g + 3]
        base = g * D_MODEL
        for c in range(0, D_MODEL, MXU_COLS):
            tq = _dot(a_q, wq_ref[:, base + c:base + c + MXU_COLS])
            tk = _dot(a_kv, wkv_ref[:, base + c:base + c + MXU_COLS])
            tv = _dot(a_kv, wkv_ref[:, Q_WIDTH + base + c:Q_WIDTH + base + c + MXU_COLS])
            for s in range(0, MXU_COLS, LANES):
                cols = slice(c + s, c + s + LANES)
                emit(q_ref, cols, rope(tq[:, s:s + LANES]) * scale)
                emit(k_ref, cols, rope(tk[:, s:s + LANES]))
            emit(v_ref, slice(c, c + MXU_COLS), tv)


def _qkv(h, pos, inv_freq, g_kv, g_q, w_kv, w_q):
    tokens, d = h.shape
    tm = QKV_TILE
    assert CHUNK % tm == 0 and tokens % CHUNK == 0
    tiles_per_chunk = CHUNK // tm
    n_chunks = tokens // CHUNK
    row = pl.BlockSpec((tm, d), lambda i: (i, 0))
    out_specs, out_shapes = [], []
    for dil in DILATIONS:
        if dil == 1:
            spec = row
            shape = jax.ShapeDtypeStruct((tokens, d), BF16)
        else:
            assert (tm // dil) % BF16_SUBLANES == 0
            spec = pl.BlockSpec((None, dil, tm // dil, d),
                                lambda i: (i // tiles_per_chunk, 0, i % tiles_per_chunk, 0))
            shape = jax.ShapeDtypeStruct((n_chunks, dil, CHUNK // dil, d), BF16)
        out_specs += [spec] * 3
        out_shapes += [shape] * 3
    outs = pl.pallas_call(
        _qkv_kernel,
        grid=(tokens // tm,),
        in_specs=[row, pl.BlockSpec((tm, 1), lambda i: (i, 0)), _resident((1, LANES)),
                  _resident((1, d)), _resident((1, d)), _resident(w_kv.shape), _resident(w_q.shape)],
        out_specs=out_specs,
        out_shape=out_shapes,
        scratch_shapes=[pltpu.VMEM((2, N_SLABS + 2, tm, LANES), F32)],
        compiler_params=_params("arbitrary"),
        name="qkv_rope",
    )(h, pos, inv_freq, g_kv, g_q, w_kv, w_q)
    dils = [dil for dil in DILATIONS for _ in range(3)]
    return [o.reshape(n_chunks, dil, CHUNK // dil, d) for o, dil in zip(outs, dils)]


ATTN_OPERANDS = 5


def _attn_kernel(*refs):
    n_in = ATTN_OPERANDS * N_BRANCHES
    in_refs, o_ref, scratch = refs[:n_in], refs[n_in], refs[n_in + 1:]
    cap_ref = scratch[-1]
    chunk = pl.program_id(2)

    lane = lax.broadcasted_iota(jnp.int32, (1, LANES), 1)
    low_head = lane < HEAD_DIM
    qi = lax.broadcasted_iota(jnp.int32, (BAND, 2 * BAND), 0)
    kj = lax.broadcasted_iota(jnp.int32, (BAND, 2 * BAND), 1)
    dist = qi + BAND - kj
    in_band = (dist >= 0) & (dist <= BAND)
    first_mask = in_band & ((kj >= BAND) | (chunk > 0))
    cap_ref[0] = jnp.where(in_band, F32_MAX, NEG_INF)
    cap_ref[1] = jnp.where(first_mask, F32_MAX, NEG_INF)
    ones = jnp.ones((2 * BAND, LANES), BF16)
    nt = (((1,), (1,)), ((), ()))

    def block(lanes, g, r, n):
        q_ref, k_ref, v_ref, ktail_ref, vtail_ref = in_refs[ATTN_OPERANDS * g:ATTN_OPERANDS * (g + 1)]
        q2 = q_ref[r, n * BAND:(n + 1) * BAND, lanes]
        if n == 0:
            k2 = jnp.concatenate([ktail_ref[r, :, lanes], k_ref[r, 0:BAND, lanes]], axis=0)
            v2 = jnp.concatenate([vtail_ref[r, :, lanes], v_ref[r, 0:BAND, lanes]], axis=0)
        else:
            k2 = k_ref[r, (n - 1) * BAND:(n + 1) * BAND, lanes]
            v2 = v_ref[r, (n - 1) * BAND:(n + 1) * BAND, lanes]
        zero = jnp.zeros_like(q2)
        qs = jnp.concatenate([jnp.where(low_head, q2, zero), jnp.where(low_head, zero, q2)], axis=0)
        s = lax.dot_general(qs, k2, nt, preferred_element_type=F32)
        cap = cap_ref[1 if n == 0 else 0]
        s = jnp.minimum(s, jnp.concatenate([cap, cap], axis=0))
        m = jnp.max(s, axis=-1, keepdims=True)
        p = jnp.exp2(s - m).astype(BF16)
        out = _dot(p, jnp.concatenate([v2, ones], axis=1))
        acc = jnp.where(low_head, out[0:BAND, 0:LANES], out[BAND:, 0:LANES])
        l = jnp.where(low_head, out[0:BAND, LANES:], out[BAND:, LANES:])
        m2 = jnp.where(low_head, m[0:BAND], m[BAND:])
        return m2, l, acc

    def merge(a, b):
        top = jnp.maximum(a[0], b[0])
        ea = jnp.exp2(a[0] - top)
        eb = jnp.exp2(b[0] - top)
        return top, ea * a[1] + eb * b[1], ea * a[2] + eb * b[2]

    for pair in range(PAIRS_PER_STEP):
        lanes = slice(pair * LANES, (pair + 1) * LANES)
        stages = (scratch[6 * pair:6 * pair + 3], scratch[6 * pair + 3:6 * pair + 6])
        for g in range(N_BRANCHES - 1, -1, -1):
            dil = DILATIONS[g]
            per_run = CHUNK // dil // BAND
            for idx in range(BLOCKS_PER_CHUNK):
                r, n = divmod(idx, per_run)
                triple = block(lanes, g, r, n)
                own_rows = slice(idx * BAND, (idx + 1) * BAND)
                if g < N_BRANCHES - 1:
                    triple = merge(triple, [ref[own_rows, :] for ref in stages[(g + 1) % 2]])
                if g > 0:
                    inner = DILATIONS[g - 1]
                    ratio = dil // inner
                    start = (r % inner) * (CHUNK // inner) + r // inner + n * BAND * ratio
                    for ref, val in zip(stages[g % 2], triple):
                        ref[pl.ds(start, BAND, stride=ratio), :] = val
                else:
                    o_ref[own_rows, lanes] = (triple[2] / triple[1]).astype(o_ref.dtype)


def _attention(qkv, batch, seq, casts):
    chunks_per_seq = seq // CHUNK

    def here(b, j, c):
        return (b * chunks_per_seq + c, 0, 0, j)

    def tail_of_previous(b, j, c, last_block):
        return (b * chunks_per_seq + jnp.maximum(c - 1, 0), 0, last_block, j)

    in_specs, operands = [], []
    for g, dil in enumerate(DILATIONS):
        run = CHUNK // dil
        cur = pl.BlockSpec((None, dil, run, PAIRS_PER_STEP * LANES), here)
        tail = pl.BlockSpec((None, dil, BAND, PAIRS_PER_STEP * LANES),
                            lambda b, j, c, last=run // BAND - 1: tail_of_previous(b, j, c, last))
        q, k, v = qkv[3 * g:3 * g + 3]
        in_specs += [cur, cur, cur, tail, tail]
        operands += [q, k, v, k, v]
    scratch = [pltpu.VMEM((CHUNK, LANES), F32)] * (6 * PAIRS_PER_STEP)
    scratch += [pltpu.VMEM((2, BAND, 2 * BAND), F32)]
    groups = N_HEAD_PAIRS // PAIRS_PER_STEP
    steps = batch * groups * chunks_per_seq
    c_in, c_out, c_shape = _cast_plan(
        casts, steps, lambda b, j, c: (b * groups + j) * chunks_per_seq + c)
    n_in = len(in_specs)
    outs = pl.pallas_call(
        _with_casts(_attn_kernel, n_in, 1, len(casts)),
        grid=(batch, groups, chunks_per_seq),
        in_specs=in_specs + c_in,
        out_specs=[pl.BlockSpec((CHUNK, PAIRS_PER_STEP * LANES),
                                lambda b, j, c: (b * chunks_per_seq + c, j))] + c_out,
        out_shape=[jax.ShapeDtypeStruct((batch * seq, D_MODEL), BF16)] + c_shape,
        scratch_shapes=scratch,
        compiler_params=_params("arbitrary", "arbitrary", "arbitrary"),
        name="dilated_attn",
    )(*operands, *[w for w, _ in casts])
    return outs[0], outs[1:]


def kernel(x, positions, mix_norm_pre, mix_norm_post, ffn_norm_pre, ffn_norm_post, ffn_w_gate_up,
           ffn_w_down, conv_w_in, conv_w, conv_w_out, kv_norm, w_kv, w_q, w_o):
    batch, seq, d = x.shape
    tokens = batch * seq
    for window, dilation in BRANCHES:
        assert window // dilation == BAND and seq % (BAND * dilation) == 0

    def gain(g):
        return g.reshape(1, d).astype(F32)

    half = HEAD_DIM // 2
    inv_freq = ROPE_THETA ** (-jnp.arange(half, dtype=F32) / half)
    inv_freq = jnp.tile(inv_freq, LANES // half).reshape(1, LANES)
    pos = positions.reshape(tokens, 1)

    h, (w_gu0, w_dn0) = _conv_mixer(
        x, gain(mix_norm_pre[0]), conv_w_in[0].astype(BF16), conv_w[0], conv_w_out[0].astype(BF16),
        gain(mix_norm_post[0]), casts=[(ffn_w_gate_up, 0), (ffn_w_down, 0)])
    h = h.reshape(tokens, d)
    h, (w_kv16, w_q16) = _ffn(h, gain(ffn_norm_pre[0]), w_gu0, w_dn0, gain(ffn_norm_post[0]),
                              casts=[(w_kv[None], 0), (w_q, 0)])

    qkv = _qkv(h, pos, inv_freq, gain(kv_norm), gain(mix_norm_pre[1]), w_kv16, w_q16)
    a, (w_gu1, w_dn1, w_o16) = _attention(
        qkv, batch, seq, casts=[(ffn_w_gate_up, 1), (ffn_w_down, 1), (w_o, 0)])
    h, _ = _ffn(h, gain(ffn_norm_pre[1]), w_gu1, w_dn1, gain(ffn_norm_post[1]),
                attn=(a, w_o16, gain(mix_norm_post[1])))
    return h.reshape(batch, seq, d)
```

```python
import math

import jax
import jax.numpy as jnp
from jax import lax
from jax.experimental import pallas as pl
from jax.experimental.pallas import tpu as pltpu

D_MODEL = 1024
HEAD_DIM = 64
N_HEADS = D_MODEL // HEAD_DIM
BRANCHES = ((128, 1), (512, 4), (2048, 16))
DILATIONS = tuple(d for _, d in BRANCHES)
N_BRANCHES = len(BRANCHES)
Q_WIDTH = N_BRANCHES * N_HEADS * HEAD_DIM
CONV_WIDTH = 3
ROPE_THETA = 10000.0
RMS_EPS = 1e-6
NEG_INF = -1e30
LOG2_E = math.log2(math.e)
F32_MAX = 3.0e38

LANES = 128
BF16_SUBLANES = 16
MXU_COLS = 256
BAND = 128
CHUNK = BAND * max(DILATIONS)
BLOCKS_PER_CHUNK = CHUNK // BAND
HEADS_PER_VREG = LANES // HEAD_DIM
N_HEAD_PAIRS = N_HEADS // HEADS_PER_VREG
N_SLABS = D_MODEL // LANES
VMEM_LIMIT_BYTES = 56 * 1024 * 1024

TOKEN_TILE = 1024
FFN_TILE = 512
QKV_TILE = 512
QKV_COLS = 2 * MXU_COLS
ROW_SPANS = 4
PAIRS_PER_STEP = 2
CARRY_ROWS = 8

F32 = jnp.float32
BF16 = jnp.bfloat16


def _dot(a, b):
    return jnp.dot(a, b, preferred_element_type=F32)


def _rms_unit(x):
    return x * lax.rsqrt(jnp.mean(x * x, axis=-1, keepdims=True) + RMS_EPS)


def _resident(shape):
    return pl.BlockSpec(shape, lambda *_: (0,) * len(shape), pipeline_mode=pl.Buffered(1))


def _params(*semantics):
    return pltpu.CompilerParams(dimension_semantics=semantics, vmem_limit_bytes=VMEM_LIMIT_BYTES)


def _cast_plan(weights, steps, linear_step):
    in_specs, out_specs, out_shapes = [], [], []
    for w, layer in weights:
        _, rows, cols = w.shape
        block = -(-rows // steps)
        block = -(-block // BF16_SUBLANES) * BF16_SUBLANES
        while rows % block:
            block += BF16_SUBLANES
        last = rows // block - 1

        def src(*idx, layer=layer, last=last):
            return (layer, jnp.minimum(linear_step(*idx), last), 0)

        def dst(*idx, last=last):
            return (jnp.minimum(linear_step(*idx), last), 0)

        in_specs.append(pl.BlockSpec((None, block, cols), src))
        out_specs.append(pl.BlockSpec((block, cols), dst))
        out_shapes.append(jax.ShapeDtypeStruct((rows, cols), BF16))
    return in_specs, out_specs, out_shapes


def _with_casts(body, n_in, n_out, n_cast):
    def kernel(*refs):
        ins, refs = refs[:n_in], refs[n_in:]
        cast_src, refs = refs[:n_cast], refs[n_cast:]
        outs, refs = refs[:n_out], refs[n_out:]
        cast_dst, scratch = refs[:n_cast], refs[n_cast:]
        body(*ins, *outs, *scratch)
        for s_ref, d_ref in zip(cast_src, cast_dst):
            d_ref[...] = s_ref[...].astype(BF16)
    return kernel


def _conv_mixer_kernel(x_ref, gpre_ref, win_ref, cw_ref, wout_ref, gpost_ref, o_ref, ubuf_ref):
    j = pl.program_id(1)
    tm = x_ref.shape[0]

    @pl.when(j == 0)
    def _():
        ubuf_ref[0:CARRY_ROWS, :] = jnp.zeros((CARRY_ROWS, D_MODEL), F32)

    @pl.when(j > 0)
    def _():
        ubuf_ref[0:CARRY_ROWS, :] = ubuf_ref[tm:tm + CARRY_ROWS, :]

    span = tm // ROW_SPANS
    xs = [x_ref[k * span:(k + 1) * span, :] for k in range(ROW_SPANS)]
    hns = [(_rms_unit(x) * gpre_ref[...]).astype(BF16) for x in xs]
    gates = [_dot(hn, win_ref[:, 0:D_MODEL]) for hn in hns]
    us = [_dot(hn, win_ref[:, D_MODEL:2 * D_MODEL]) * _dot(hn, win_ref[:, 2 * D_MODEL:3 * D_MODEL])
          for hn in hns]
    ys = []
    for k, (gate, u) in enumerate(zip(gates, us)):
        base = CARRY_ROWS + k * span
        ubuf_ref[base:base + span, :] = u
        u1 = ubuf_ref[base - 1:base - 1 + span, :]
        u2 = ubuf_ref[base - 2:base - 2 + span, :]
        conv = cw_ref[0:1, :] * u2 + cw_ref[1:2, :] * u1 + cw_ref[2:3, :] * u
        ys.append(_dot((gate * conv).astype(BF16), wout_ref[...]))
    for k, (x, y) in enumerate(zip(xs, ys)):
        o_ref[k * span:(k + 1) * span, :] = x + _rms_unit(y) * gpost_ref[...]


def _conv_mixer(x, g_pre, w_in, conv_w, w_out, g_post, casts):
    batch, seq, d = x.shape
    tm = TOKEN_TILE
    tiles = seq // tm
    row = pl.BlockSpec((None, tm, d), lambda b, j: (b, j, 0))
    c_in, c_out, c_shape = _cast_plan(casts, batch * tiles, lambda b, j: b * tiles + j)
    in_specs = [row, _resident((1, d)), _resident(w_in.shape), _resident(conv_w.shape),
                _resident(w_out.shape), _resident((1, d))]
    outs = pl.pallas_call(
        _with_casts(_conv_mixer_kernel, len(in_specs), 1, len(casts)),
        grid=(batch, tiles),
        in_specs=in_specs + c_in,
        out_specs=[row] + c_out,
        out_shape=[jax.ShapeDtypeStruct(x.shape, F32)] + c_shape,
        scratch_shapes=[pltpu.VMEM((tm + CARRY_ROWS, d), F32)],
        compiler_params=_params("arbitrary", "arbitrary"),
        name="conv_mixer",
    )(x, g_pre, w_in, conv_w, w_out, g_post, *[w for w, _ in casts])
    return outs[0], outs[1:]


def _swiglu_residual(hs, gpre_ref, wgu_ref, wd_ref, gpost_ref, act_ref):
    d_ff = wd_ref.shape[0]
    span = hs[0].shape[0]
    hns = [(_rms_unit(h) * gpre_ref[...]).astype(BF16) for h in hs]
    for c in range(0, d_ff, MXU_COLS):
        for k, hn in enumerate(hns):
            g = _dot(hn, wgu_ref[:, c:c + MXU_COLS])
            u = _dot(hn, wgu_ref[:, d_ff + c:d_ff + c + MXU_COLS])
            act_ref[k * span:(k + 1) * span, c:c + MXU_COLS] = (g * jax.nn.sigmoid(g) * u).astype(BF16)
    fs = [_dot(act_ref[k * span:(k + 1) * span, :], wd_ref[...]) for k in range(len(hs))]
    return [h + _rms_unit(f) * gpost_ref[...] for h, f in zip(hs, fs)]


def _row_spans(ref):
    span = ref.shape[0] // ROW_SPANS
    return [ref[k * span:(k + 1) * span, :] for k in range(ROW_SPANS)]


def _ffn_kernel(h_ref, gpre_ref, wgu_ref, wd_ref, gpost_ref, o_ref, act_ref):
    outs = _swiglu_residual(_row_spans(h_ref), gpre_ref, wgu_ref, wd_ref, gpost_ref, act_ref)
    o_ref[...] = jnp.concatenate(outs, axis=0)


def _attn_out_ffn_kernel(h_ref, a_ref, wo_ref, gmix_ref, gpre_ref, wgu_ref, wd_ref, gpost_ref,
                         o_ref, act_ref):
    hs = [h + _rms_unit(_dot(a, wo_ref[...])) * gmix_ref[...]
          for h, a in zip(_row_spans(h_ref), _row_spans(a_ref))]
    outs = _swiglu_residual(hs, gpre_ref, wgu_ref, wd_ref, gpost_ref, act_ref)
    o_ref[...] = jnp.concatenate(outs, axis=0)


def _ffn(h, g_pre, w_gate_up, w_down, g_post, attn=None, casts=()):
    tokens, d = h.shape
    d_ff = w_down.shape[0]
    assert d_ff % MXU_COLS == 0
    tm = FFN_TILE if attn is None else TOKEN_TILE
    row = pl.BlockSpec((tm, d), lambda i: (i, 0))
    ffn_specs = [_resident((1, d)), _resident(w_gate_up.shape), _resident(w_down.shape),
                 _resident((1, d))]
    if attn is None:
        body, specs, args = _ffn_kernel, [row], (h,)
    else:
        a, w_o, g_mix = attn
        body = _attn_out_ffn_kernel
        specs = [row, row, _resident(w_o.shape), _resident((1, d))]
        args = (h, a, w_o, g_mix)
    in_specs = specs + ffn_specs
    c_in, c_out, c_shape = _cast_plan(casts, tokens // tm, lambda i: i)
    outs = pl.pallas_call(
        _with_casts(body, len(in_specs), 1, len(casts)),
        grid=(tokens // tm,),
        in_specs=in_specs + c_in,
        out_specs=[row] + c_out,
        out_shape=[jax.ShapeDtypeStruct(h.shape, F32)] + c_shape,
        scratch_shapes=[pltpu.VMEM((tm, d_ff), BF16)],
        compiler_params=_params("arbitrary"),
        name="ffn" if attn is None else "attn_out_ffn",
    )(*args, g_pre, w_gate_up, w_down, g_post, *[w for w, _ in casts])
    return outs[0], outs[1:]


def _qkv_kernel(h_ref, pos_ref, invf_ref, gkv_ref, gq_ref, wkv_ref, wq_ref, *rest):
    out_refs, stage_ref = rest[:3 * N_BRANCHES], rest[3 * N_BRANCHES]
    tm = h_ref.shape[0]

    for s in range(N_SLABS):
        stage_ref[0, s] = h_ref[:, s * LANES:(s + 1) * LANES]

    lane = lax.broadcasted_iota(jnp.int32, (1, LANES), 1)
    half = HEAD_DIM // 2
    groups = LANES // half
    packed = tm // groups
    group = lane // half
    pos = pos_ref[...].astype(F32)
    pos_packed = jnp.broadcast_to(pos[0:packed], (packed, LANES))
    for j in range(1, groups):
        pos_packed = jnp.where(group == j, pos[j * packed:(j + 1) * packed], pos_packed)
    ang = pos_packed * invf_ref[...]
    for t, table in enumerate((jnp.cos(ang), jnp.sin(ang))):
        for j in range(groups):
            x = jnp.where(group == j, table, 0.0)
            shift = LANES // 2
            while shift >= half:
                x = x + pltpu.roll(x, shift, axis=1)
                shift //= 2
            stage_ref[0, N_SLABS + t, j * packed:(j + 1) * packed, :] = x

    first_half = (lane % HEAD_DIM) < (HEAD_DIM // 2)
    scale = HEAD_DIM ** -0.5 * LOG2_E

    prepared = []
    for g, dil in enumerate(DILATIONS):
        rows = tm // dil
        src = stage_ref.at[(g - 1) % 2] if g else stage_ref.at[0]
        if g == 0:
            slabs = [src[s] for s in range(N_SLABS + 2)]
        else:
            inner = DILATIONS[g - 1]
            ratio = dil // inner
            starts = [(r % inner) * (tm // inner) + r // inner for r in range(dil)]
            slabs = [jnp.concatenate([src[s, pl.ds(st, rows, stride=ratio), :] for st in starts], axis=0)
                     for s in range(N_SLABS + 2)]
            if g < N_BRANCHES - 1:
                for s, slab in enumerate(slabs):
                    stage_ref[g % 2, s] = slab
        y = _rms_unit(jnp.concatenate(slabs[:N_SLABS], axis=1))
        cos, sin = slabs[N_SLABS:]
        prepared.append(((y * gq_ref[...]).astype(BF16), (y * gkv_ref[...]).astype(BF16),
                         cos, jnp.where(first_half, -sin, sin)))

    for g, dil in enumerate(DILATIONS):
        rows = tm // dil
        a_q, a_kv, cos, sin_signed = prepared[g]

        def rope(t):
            partner = jnp.where(first_half,
                                pltpu.roll(t, LANES - HEAD_DIM // 2, axis=1),
                                pltpu.roll(t, HEAD_DIM // 2, axis=1))
            return t * cos + partner * sin_signed

        def emit(ref, cols, val):
            if dil == 1:
                ref[:, cols] = val.astype(BF16)
            else:
                for r in range(dil):
                    ref[r, :, cols] = val[r * rows:(r + 1) * rows].astype(BF16)

        q_ref, k_ref, v_ref = out_refs[3 * g:3 * g + 3]
        base = g * D_MODEL
        for c in range(0, D_MODEL, QKV_COLS):
            tq = _dot(a_q, wq_ref[:, base + c:base + c + QKV_COLS])
            tk = _dot(a_kv, wkv_ref[:, base + c:base + c + QKV_COLS])
            tv = _dot(a_kv, wkv_ref[:, Q_WIDTH + base + c:Q_WIDTH + base + c + QKV_COLS])
            for s in range(0, QKV_COLS, LANES):
                cols = slice(c + s, c + s + LANES)
                emit(q_ref, cols, rope(tq[:, s:s + LANES]) * scale)
                emit(k_ref, cols, rope(tk[:, s:s + LANES]))
            emit(v_ref, slice(c, c + QKV_COLS), tv)


def _qkv(h, pos, inv_freq, g_kv, g_q, w_kv, w_q):
    tokens, d = h.shape
    tm = QKV_TILE
    assert CHUNK % tm == 0 and tokens % CHUNK == 0
    tiles_per_chunk = CHUNK // tm
    n_chunks = tokens // CHUNK
    row = pl.BlockSpec((tm, d), lambda i: (i, 0))
    out_specs, out_shapes = [], []
    for dil in DILATIONS:
        if dil == 1:
            spec = row
            shape = jax.ShapeDtypeStruct((tokens, d), BF16)
        else:
            assert (tm // dil) % BF16_SUBLANES == 0
            spec = pl.BlockSpec((None, dil, tm // dil, d),
                                lambda i: (i // tiles_per_chunk, 0, i % tiles_per_chunk, 0))
            shape = jax.ShapeDtypeStruct((n_chunks, dil, CHUNK // dil, d), BF16)
        out_specs += [spec] * 3
        out_shapes += [shape] * 3
    outs = pl.pallas_call(
        _qkv_kernel,
        grid=(tokens // tm,),
        in_specs=[row, pl.BlockSpec((tm, 1), lambda i: (i, 0)), _resident((1, LANES)),
                  _resident((1, d)), _resident((1, d)), _resident(w_kv.shape), _resident(w_q.shape)],
        out_specs=out_specs,
        out_shape=out_shapes,
        scratch_shapes=[pltpu.VMEM((2, N_SLABS + 2, tm, LANES), F32)],
        compiler_params=_params("arbitrary"),
        name="qkv_rope",
    )(h, pos, inv_freq, g_kv, g_q, w_kv, w_q)
    dils = [dil for dil in DILATIONS for _ in range(3)]
    return [o.reshape(n_chunks, dil, CHUNK // dil, d) for o, dil in zip(outs, dils)]


ATTN_OPERANDS = 5


def _attn_kernel(*refs):
    n_in = ATTN_OPERANDS * N_BRANCHES
    in_refs, o_ref, scratch = refs[:n_in], refs[n_in], refs[n_in + 1:]
    cap_ref = scratch[-1]
    chunk = pl.program_id(2)

    lane = lax.broadcasted_iota(jnp.int32, (1, LANES), 1)
    low_head = lane < HEAD_DIM
    qi = lax.broadcasted_iota(jnp.int32, (BAND, 2 * BAND), 0)
    kj = lax.broadcasted_iota(jnp.int32, (BAND, 2 * BAND), 1)
    dist = qi + BAND - kj
    in_band = (dist >= 0) & (dist <= BAND)
    first_mask = in_band & ((kj >= BAND) | (chunk > 0))
    cap_ref[0] = jnp.where(in_band, F32_MAX, NEG_INF)
    cap_ref[1] = jnp.where(first_mask, F32_MAX, NEG_INF)
    ones = jnp.ones((2 * BAND, LANES), BF16)
    nt = (((1,), (1,)), ((), ()))

    def block(lanes, g, r, n):
        q_ref, k_ref, v_ref, ktail_ref, vtail_ref = in_refs[ATTN_OPERANDS * g:ATTN_OPERANDS * (g + 1)]
        q2 = q_ref[r, n * BAND:(n + 1) * BAND, lanes]
        if n == 0:
            k2 = jnp.concatenate([ktail_ref[r, :, lanes], k_ref[r, 0:BAND, lanes]], axis=0)
            v2 = jnp.concatenate([vtail_ref[r, :, lanes], v_ref[r, 0:BAND, lanes]], axis=0)
        else:
            k2 = k_ref[r, (n - 1) * BAND:(n + 1) * BAND, lanes]
            v2 = v_ref[r, (n - 1) * BAND:(n + 1) * BAND, lanes]
        zero = jnp.zeros_like(q2)
        qs = jnp.concatenate([jnp.where(low_head, q2, zero), jnp.where(low_head, zero, q2)], axis=0)
        s = lax.dot_general(qs, k2, nt, preferred_element_type=F32)
        cap = cap_ref[1 if n == 0 else 0]
        s = jnp.minimum(s, jnp.concatenate([cap, cap], axis=0))
        m = jnp.max(s, axis=-1, keepdims=True)
        p = jnp.exp2(s - m).astype(BF16)
        out = _dot(p, jnp.concatenate([v2, ones], axis=1))
        acc = jnp.where(low_head, out[0:BAND, 0:LANES], out[BAND:, 0:LANES])
        l = jnp.where(low_head, out[0:BAND, LANES:], out[BAND:, LANES:])
        m2 = jnp.where(low_head, m[0:BAND], m[BAND:])
        return m2, l, acc

    def merge(a, b):
        top = jnp.maximum(a[0], b[0])
        ea = jnp.exp2(a[0] - top)
        eb = jnp.exp2(b[0] - top)
        return top, ea * a[1] + eb * b[1], ea * a[2] + eb * b[2]

    for pair in range(PAIRS_PER_STEP):
        lanes = slice(pair * LANES, (pair + 1) * LANES)
        stages = (scratch[6 * pair:6 * pair + 3], scratch[6 * pair + 3:6 * pair + 6])
        for g in range(N_BRANCHES - 1, -1, -1):
            dil = DILATIONS[g]
            per_run = CHUNK // dil // BAND
            for idx in range(BLOCKS_PER_CHUNK):
                r, n = divmod(idx, per_run)
                triple = block(lanes, g, r, n)
                own_rows = slice(idx * BAND, (idx + 1) * BAND)
                if g < N_BRANCHES - 1:
                    triple = merge(triple, [ref[own_rows, :] for ref in stages[(g + 1) % 2]])
                if g > 0:
                    inner = DILATIONS[g - 1]
                    ratio = dil // inner
                    start = (r % inner) * (CHUNK // inner) + r // inner + n * BAND * ratio
                    for ref, val in zip(stages[g % 2], triple):
                        ref[pl.ds(start, BAND, stride=ratio), :] = val
                else:
                    o_ref[own_rows, lanes] = (triple[2] / triple[1]).astype(o_ref.dtype)


def _attention(qkv, batch, seq, casts):
    chunks_per_seq = seq // CHUNK

    def here(b, j, c):
        return (b * chunks_per_seq + c, 0, 0, j)

    def tail_of_previous(b, j, c, last_block):
        return (b * chunks_per_seq + jnp.maximum(c - 1, 0), 0, last_block, j)

    in_specs, operands = [], []
    for g, dil in enumerate(DILATIONS):
        run = CHUNK // dil
        cur = pl.BlockSpec((None, dil, run, PAIRS_PER_STEP * LANES), here)
        tail = pl.BlockSpec((None, dil, BAND, PAIRS_PER_STEP * LANES),
                            lambda b, j, c, last=run // BAND - 1: tail_of_previous(b, j, c, last))
        q, k, v = qkv[3 * g:3 * g + 3]
        in_specs += [cur, cur, cur, tail, tail]
        operands += [q, k, v, k, v]
    scratch = [pltpu.VMEM((CHUNK, LANES), F32)] * (6 * PAIRS_PER_STEP)
    scratch += [pltpu.VMEM((2, BAND, 2 * BAND), F32)]
    groups = N_HEAD_PAIRS // PAIRS_PER_STEP
    steps = batch * groups * chunks_per_seq
    c_in, c_out, c_shape = _cast_plan(
        casts, steps, lambda b, j, c: (b * groups + j) * chunks_per_seq + c)
    n_in = len(in_specs)
    outs = pl.pallas_call(
        _with_casts(_attn_kernel, n_in, 1, len(casts)),
        grid=(batch, groups, chunks_per_seq),
        in_specs=in_specs + c_in,
        out_specs=[pl.BlockSpec((CHUNK, PAIRS_PER_STEP * LANES),
                                lambda b, j, c: (b * chunks_per_seq + c, j))] + c_out,
        out_shape=[jax.ShapeDtypeStruct((batch * seq, D_MODEL), BF16)] + c_shape,
        scratch_shapes=scratch,
        compiler_params=_params("arbitrary", "arbitrary", "arbitrary"),
        name="dilated_attn",
    )(*operands, *[w for w, _ in casts])
    return outs[0], outs[1:]


def kernel(x, positions, mix_norm_pre, mix_norm_post, ffn_norm_pre, ffn_norm_post, ffn_w_gate_up,
           ffn_w_down, conv_w_in, conv_w, conv_w_out, kv_norm, w_kv, w_q, w_o):
    batch, seq, d = x.shape
    tokens = batch * seq
    for window, dilation in BRANCHES:
        assert window // dilation == BAND and seq % (BAND * dilation) == 0

    def gain(g):
        return g.reshape(1, d).astype(F32)

    half = HEAD_DIM // 2
    inv_freq = ROPE_THETA ** (-jnp.arange(half, dtype=F32) / half)
    inv_freq = jnp.tile(inv_freq, LANES // half).reshape(1, LANES)
    pos = positions.reshape(tokens, 1)

    h, (w_gu0, w_dn0) = _conv_mixer(
        x, gain(mix_norm_pre[0]), conv_w_in[0].astype(BF16), conv_w[0], conv_w_out[0].astype(BF16),
        gain(mix_norm_post[0]), casts=[(ffn_w_gate_up, 0), (ffn_w_down, 0)])
    h = h.reshape(tokens, d)
    h, (w_kv16, w_q16) = _ffn(h, gain(ffn_norm_pre[0]), w_gu0, w_dn0, gain(ffn_norm_post[0]),
                              casts=[(w_kv[None], 0), (w_q, 0)])

    qkv = _qkv(h, pos, inv_freq, gain(kv_norm), gain(mix_norm_pre[1]), w_kv16, w_q16)
    a, (w_gu1, w_dn1, w_o16) = _attention(
        qkv, batch, seq, casts=[(ffn_w_gate_up, 1), (ffn_w_down, 1), (w_o, 0)])
    h, _ = _ffn(h, gain(ffn_norm_pre[1]), w_gu1, w_dn1, gain(ffn_norm_post[1]),
                attn=(a, w_o16, gain(mix_norm_post[1])))
    return h.reshape(batch, seq, d)
```

```python
import math

import jax
import jax.numpy as jnp
from jax import lax
from jax.experimental import pallas as pl
from jax.experimental.pallas import tpu as pltpu

D_MODEL = 1024
HEAD_DIM = 64
N_HEADS = D_MODEL // HEAD_DIM
BRANCHES = ((128, 1), (512, 4), (2048, 16))
DILATIONS = tuple(d for _, d in BRANCHES)
N_BRANCHES = len(BRANCHES)
Q_WIDTH = N_BRANCHES * N_HEADS * HEAD_DIM
CONV_WIDTH = 3
ROPE_THETA = 10000.0
RMS_EPS = 1e-6
NEG_INF = -1e30
LOG2_E = math.log2(math.e)
F32_MAX = 3.0e38

LANES = 128
BF16_SUBLANES = 16
MXU_COLS = 256
BAND = 128
CHUNK = BAND * max(DILATIONS)
BLOCKS_PER_CHUNK = CHUNK // BAND
HEADS_PER_VREG = LANES // HEAD_DIM
N_HEAD_PAIRS = N_HEADS // HEADS_PER_VREG
N_SLABS = D_MODEL // LANES
VMEM_LIMIT_BYTES = 56 * 1024 * 1024

TOKEN_TILE = 1024
QKV_TILE = 512
QKV_COLS = 2 * MXU_COLS
ROW_SPANS = 4
PAIRS_PER_STEP = 2
CARRY_ROWS = 8

F32 = jnp.float32
BF16 = jnp.bfloat16


def _dot(a, b):
    return jnp.dot(a, b, preferred_element_type=F32)


def _rms_unit(x):
    return x * lax.rsqrt(jnp.mean(x * x, axis=-1, keepdims=True) + RMS_EPS)


def _resident(shape):
    return pl.BlockSpec(shape, lambda *_: (0,) * len(shape), pipeline_mode=pl.Buffered(1))


def _params(*semantics, fusible_inputs=None):
    return pltpu.CompilerParams(dimension_semantics=semantics, vmem_limit_bytes=VMEM_LIMIT_BYTES,
                                allow_input_fusion=fusible_inputs)


def _cast_plan(weights, steps, linear_step):
    in_specs, out_specs, out_shapes = [], [], []
    for w, layer in weights:
        _, rows, cols = w.shape
        block = -(-rows // steps)
        block = -(-block // BF16_SUBLANES) * BF16_SUBLANES
        while rows % block:
            block += BF16_SUBLANES
        last = rows // block - 1

        def src(*idx, layer=layer, last=last):
            return (layer, jnp.minimum(linear_step(*idx), last), 0)

        def dst(*idx, last=last):
            return (jnp.minimum(linear_step(*idx), last), 0)

        in_specs.append(pl.BlockSpec((None, block, cols), src))
        out_specs.append(pl.BlockSpec((block, cols), dst))
        out_shapes.append(jax.ShapeDtypeStruct((rows, cols), BF16))
    return in_specs, out_specs, out_shapes


def _with_casts(body, n_in, n_out, n_cast):
    def kernel(*refs):
        ins, refs = refs[:n_in], refs[n_in:]
        cast_src, refs = refs[:n_cast], refs[n_cast:]
        outs, refs = refs[:n_out], refs[n_out:]
        cast_dst, scratch = refs[:n_cast], refs[n_cast:]
        body(*ins, *outs, *scratch)
        for s_ref, d_ref in zip(cast_src, cast_dst):
            d_ref[...] = s_ref[...].astype(BF16)
    return kernel


def _conv_mixer_kernel(x_ref, gpre_ref, win_ref, cw_ref, wout_ref, gpost_ref, o_ref, ubuf_ref):
    j = pl.program_id(1)
    tm = x_ref.shape[0]

    @pl.when(j == 0)
    def _():
        ubuf_ref[0:CARRY_ROWS, :] = jnp.zeros((CARRY_ROWS, D_MODEL), F32)

    @pl.when(j > 0)
    def _():
        ubuf_ref[0:CARRY_ROWS, :] = ubuf_ref[tm:tm + CARRY_ROWS, :]

    span = tm // ROW_SPANS
    xs = [x_ref[k * span:(k + 1) * span, :] for k in range(ROW_SPANS)]
    hns = [(_rms_unit(x) * gpre_ref[...]).astype(BF16) for x in xs]
    gates = [_dot(hn, win_ref[:, 0:D_MODEL]) for hn in hns]
    us = [_dot(hn, win_ref[:, D_MODEL:2 * D_MODEL]) * _dot(hn, win_ref[:, 2 * D_MODEL:3 * D_MODEL])
          for hn in hns]
    ys = []
    for k, (gate, u) in enumerate(zip(gates, us)):
        base = CARRY_ROWS + k * span
        ubuf_ref[base:base + span, :] = u
        u1 = ubuf_ref[base - 1:base - 1 + span, :]
        u2 = ubuf_ref[base - 2:base - 2 + span, :]
        conv = cw_ref[0:1, :] * u2 + cw_ref[1:2, :] * u1 + cw_ref[2:3, :] * u
        ys.append(_dot((gate * conv).astype(BF16), wout_ref[...]))
    for k, (x, y) in enumerate(zip(xs, ys)):
        o_ref[k * span:(k + 1) * span, :] = x + _rms_unit(y) * gpost_ref[...]


def _conv_mixer(x, g_pre, w_in, conv_w, w_out, g_post, casts):
    batch, seq, d = x.shape
    tm = TOKEN_TILE
    tiles = seq // tm
    row = pl.BlockSpec((None, tm, d), lambda b, j: (b, j, 0))
    c_in, c_out, c_shape = _cast_plan(casts, batch * tiles, lambda b, j: b * tiles + j)
    in_specs = [row, _resident((1, d)), _resident(w_in.shape), _resident(conv_w.shape),
                _resident(w_out.shape), _resident((1, d))]
    outs = pl.pallas_call(
        _with_casts(_conv_mixer_kernel, len(in_specs), 1, len(casts)),
        grid=(batch, tiles),
        in_specs=in_specs + c_in,
        out_specs=[row] + c_out,
        out_shape=[jax.ShapeDtypeStruct(x.shape, F32)] + c_shape,
        scratch_shapes=[pltpu.VMEM((tm + CARRY_ROWS, d), F32)],
        compiler_params=_params("arbitrary", "arbitrary",
                                fusible_inputs=[False] + [True] * 5 + [False] * len(casts)),
        name="conv_mixer",
    )(x, g_pre, w_in, conv_w, w_out, g_post, *[w for w, _ in casts])
    return outs[0], outs[1:]


def _swiglu_residual(hs, gpre_ref, wgu_ref, wd_ref, gpost_ref, act_ref):
    d_ff = wd_ref.shape[0]
    span = hs[0].shape[0]
    hns = [(_rms_unit(h) * gpre_ref[...]).astype(BF16) for h in hs]
    for c in range(0, d_ff, MXU_COLS):
        for k, hn in enumerate(hns):
            g = _dot(hn, wgu_ref[:, c:c + MXU_COLS])
            u = _dot(hn, wgu_ref[:, d_ff + c:d_ff + c + MXU_COLS])
            act_ref[k * span:(k + 1) * span, c:c + MXU_COLS] = (g * jax.nn.sigmoid(g) * u).astype(BF16)
    fs = [_dot(act_ref[k * span:(k + 1) * span, :], wd_ref[...]) for k in range(len(hs))]
    return [h + _rms_unit(f) * gpost_ref[...] for h, f in zip(hs, fs)]


def _row_spans(ref):
    span = ref.shape[0] // ROW_SPANS
    return [ref[k * span:(k + 1) * span, :] for k in range(ROW_SPANS)]


def _ffn_kernel(h_ref, gpre_ref, wgu_ref, wd_ref, gpost_ref, o_ref, act_ref):
    outs = _swiglu_residual(_row_spans(h_ref), gpre_ref, wgu_ref, wd_ref, gpost_ref, act_ref)
    o_ref[...] = jnp.concatenate(outs, axis=0)


def _attn_out_ffn_kernel(h_ref, a_ref, wo_ref, gmix_ref, gpre_ref, wgu_ref, wd_ref, gpost_ref,
                         o_ref, act_ref):
    hs = [h + _rms_unit(_dot(a, wo_ref[...])) * gmix_ref[...]
          for h, a in zip(_row_spans(h_ref), _row_spans(a_ref))]
    outs = _swiglu_residual(hs, gpre_ref, wgu_ref, wd_ref, gpost_ref, act_ref)
    o_ref[...] = jnp.concatenate(outs, axis=0)


def _ffn(h, g_pre, w_gate_up, w_down, g_post, attn=None, casts=()):
    tokens, d = h.shape
    d_ff = w_down.shape[0]
    assert d_ff % MXU_COLS == 0
    tm = TOKEN_TILE
    row = pl.BlockSpec((tm, d), lambda i: (i, 0))
    ffn_specs = [_resident((1, d)), _resident(w_gate_up.shape), _resident(w_down.shape),
                 _resident((1, d))]
    if attn is None:
        body, specs, args = _ffn_kernel, [row], (h,)
    else:
        a, w_o, g_mix = attn
        body = _attn_out_ffn_kernel
        specs = [row, row, _resident(w_o.shape), _resident((1, d))]
        args = (h, a, w_o, g_mix)
    in_specs = specs + ffn_specs
    c_in, c_out, c_shape = _cast_plan(casts, tokens // tm, lambda i: i)
    outs = pl.pallas_call(
        _with_casts(body, len(in_specs), 1, len(casts)),
        grid=(tokens // tm,),
        in_specs=in_specs + c_in,
        out_specs=[row] + c_out,
        out_shape=[jax.ShapeDtypeStruct(h.shape, F32)] + c_shape,
        scratch_shapes=[pltpu.VMEM((tm, d_ff), BF16)],
        compiler_params=_params("arbitrary"),
        name="ffn" if attn is None else "attn_out_ffn",
    )(*args, g_pre, w_gate_up, w_down, g_post, *[w for w, _ in casts])
    return outs[0], outs[1:]


def _qkv_kernel(h_ref, pos_ref, invf_ref, gkv_ref, gq_ref, wkv_ref, wq_ref, *rest):
    out_refs, stage_ref = rest[:3 * N_BRANCHES], rest[3 * N_BRANCHES]
    tm = h_ref.shape[0]

    for s in range(N_SLABS):
        stage_ref[0, s] = h_ref[:, s * LANES:(s + 1) * LANES]

    lane = lax.broadcasted_iota(jnp.int32, (1, LANES), 1)
    half = HEAD_DIM // 2
    groups = LANES // half
    packed = tm // groups
    group = lane // half
    pos = pos_ref[...].astype(F32)
    pos_packed = jnp.broadcast_to(pos[0:packed], (packed, LANES))
    for j in range(1, groups):
        pos_packed = jnp.where(group == j, pos[j * packed:(j + 1) * packed], pos_packed)
    ang = pos_packed * invf_ref[...]
    for t, table in enumerate((jnp.cos(ang), jnp.sin(ang))):
        for j in range(groups):
            x = jnp.where(group == j, table, 0.0)
            shift = LANES // 2
            while shift >= half:
                x = x + pltpu.roll(x, shift, axis=1)
                shift //= 2
            stage_ref[0, N_SLABS + t, j * packed:(j + 1) * packed, :] = x

    first_half = (lane % HEAD_DIM) < (HEAD_DIM // 2)
    scale = HEAD_DIM ** -0.5 * LOG2_E

    prepared = []
    for g, dil in enumerate(DILATIONS):
        rows = tm // dil
        src = stage_ref.at[(g - 1) % 2] if g else stage_ref.at[0]
        if g == 0:
            slabs = [src[s] for s in range(N_SLABS + 2)]
        else:
            inner = DILATIONS[g - 1]
            ratio = dil // inner
            starts = [(r % inner) * (tm // inner) + r // inner for r in range(dil)]
            slabs = [jnp.concatenate([src[s, pl.ds(st, rows, stride=ratio), :] for st in starts], axis=0)
                     for s in range(N_SLABS + 2)]
            if g < N_BRANCHES - 1:
                for s, slab in enumerate(slabs):
                    stage_ref[g % 2, s] = slab
        y = _rms_unit(jnp.concatenate(slabs[:N_SLABS], axis=1))
        cos, sin = slabs[N_SLABS:]
        prepared.append(((y * gq_ref[...]).astype(BF16), (y * gkv_ref[...]).astype(BF16),
                         cos, jnp.where(first_half, -sin, sin)))

    for g, dil in enumerate(DILATIONS):
        rows = tm // dil
        a_q, a_kv, cos, sin_signed = prepared[g]

        def rope(t):
            partner = jnp.where(first_half,
                                pltpu.roll(t, LANES - HEAD_DIM // 2, axis=1),
                                pltpu.roll(t, HEAD_DIM // 2, axis=1))
            return t * cos + partner * sin_signed

        def emit(ref, cols, val):
            if dil == 1:
                ref[:, cols] = val.astype(BF16)
            else:
                for r in range(dil):
                    ref[r, :, cols] = val[r * rows:(r + 1) * rows].astype(BF16)

        q_ref, k_ref, v_ref = out_refs[3 * g:3 * g + 3]
        base = g * D_MODEL
        for c in range(0, D_MODEL, QKV_COLS):
            tq = _dot(a_q, wq_ref[:, base + c:base + c + QKV_COLS])
            tk = _dot(a_kv, wkv_ref[:, base + c:base + c + QKV_COLS])
            tv = _dot(a_kv, wkv_ref[:, Q_WIDTH + base + c:Q_WIDTH + base + c + QKV_COLS])
            for s in range(0, QKV_COLS, LANES):
                cols = slice(c + s, c + s + LANES)
                emit(q_ref, cols, rope(tq[:, s:s + LANES]) * scale)
                emit(k_ref, cols, rope(tk[:, s:s + LANES]))
            emit(v_ref, slice(c, c + QKV_COLS), tv)


def _qkv(h, pos, inv_freq, g_kv, g_q, w_kv, w_q):
    tokens, d = h.shape
    tm = QKV_TILE
    assert CHUNK % tm == 0 and tokens % CHUNK == 0
    tiles_per_chunk = CHUNK // tm
    n_chunks = tokens // CHUNK
    row = pl.BlockSpec((tm, d), lambda i: (i, 0))
    out_specs, out_shapes = [], []
    for dil in DILATIONS:
        if dil == 1:
            spec = row
            shape = jax.ShapeDtypeStruct((tokens, d), BF16)
        else:
            assert (tm // dil) % BF16_SUBLANES == 0
            spec = pl.BlockSpec((None, dil, tm // dil, d),
                                lambda i: (i // tiles_per_chunk, 0, i % tiles_per_chunk, 0))
            shape = jax.ShapeDtypeStruct((n_chunks, dil, CHUNK // dil, d), BF16)
        out_specs += [spec] * 3
        out_shapes += [shape] * 3
    outs = pl.pallas_call(
        _qkv_kernel,
        grid=(tokens // tm,),
        in_specs=[row, pl.BlockSpec((tm, 1), lambda i: (i, 0)), _resident((1, LANES)),
                  _resident((1, d)), _resident((1, d)), _resident(w_kv.shape), _resident(w_q.shape)],
        out_specs=out_specs,
        out_shape=out_shapes,
        scratch_shapes=[pltpu.VMEM((2, N_SLABS + 2, tm, LANES), F32)],
        compiler_params=_params("arbitrary", fusible_inputs=[False, True, True, True, True, False, False]),
        name="qkv_rope",
    )(h, pos, inv_freq, g_kv, g_q, w_kv, w_q)
    dils = [dil for dil in DILATIONS for _ in range(3)]
    return [o.reshape(n_chunks, dil, CHUNK // dil, d) for o, dil in zip(outs, dils)]


ATTN_OPERANDS = 5


def _attn_kernel(*refs):
    n_in = ATTN_OPERANDS * N_BRANCHES
    in_refs, o_ref, scratch = refs[:n_in], refs[n_in], refs[n_in + 1:]
    cap_ref = scratch[-1]
    chunk = pl.program_id(2)

    lane = lax.broadcasted_iota(jnp.int32, (1, LANES), 1)
    low_head = lane < HEAD_DIM
    qi = lax.broadcasted_iota(jnp.int32, (BAND, 2 * BAND), 0)
    kj = lax.broadcasted_iota(jnp.int32, (BAND, 2 * BAND), 1)
    dist = qi + BAND - kj
    in_band = (dist >= 0) & (dist <= BAND)
    first_mask = in_band & ((kj >= BAND) | (chunk > 0))
    cap_ref[0] = jnp.where(in_band, F32_MAX, NEG_INF)
    cap_ref[1] = jnp.where(first_mask, F32_MAX, NEG_INF)
    ones = jnp.ones((2 * BAND, LANES), BF16)
    nt = (((1,), (1,)), ((), ()))

    def block(lanes, g, r, n):
        q_ref, k_ref, v_ref, ktail_ref, vtail_ref = in_refs[ATTN_OPERANDS * g:ATTN_OPERANDS * (g + 1)]
        q2 = q_ref[r, n * BAND:(n + 1) * BAND, lanes]
        if n == 0:
            k2 = jnp.concatenate([ktail_ref[r, :, lanes], k_ref[r, 0:BAND, lanes]], axis=0)
            v2 = jnp.concatenate([vtail_ref[r, :, lanes], v_ref[r, 0:BAND, lanes]], axis=0)
        else:
            k2 = k_ref[r, (n - 1) * BAND:(n + 1) * BAND, lanes]
            v2 = v_ref[r, (n - 1) * BAND:(n + 1) * BAND, lanes]
        zero = jnp.zeros_like(q2)
        qs = jnp.concatenate([jnp.where(low_head, q2, zero), jnp.where(low_head, zero, q2)], axis=0)
        s = lax.dot_general(qs, k2, nt, preferred_element_type=F32)
        cap = cap_ref[1 if n == 0 else 0]
        s = jnp.minimum(s, jnp.concatenate([cap, cap], axis=0))
        m = jnp.max(s, axis=-1, keepdims=True)
        p = jnp.exp2(s - m).astype(BF16)
        out = _dot(p, jnp.concatenate([v2, ones], axis=1))
        acc = jnp.where(low_head, out[0:BAND, 0:LANES], out[BAND:, 0:LANES])
        l = jnp.where(low_head, out[0:BAND, LANES:], out[BAND:, LANES:])
        m2 = jnp.where(low_head, m[0:BAND], m[BAND:])
        return m2, l, acc

    def merge(a, b):
        top = jnp.maximum(a[0], b[0])
        ea = jnp.exp2(a[0] - top)
        eb = jnp.exp2(b[0] - top)
        return top, ea * a[1] + eb * b[1], ea * a[2] + eb * b[2]

    for pair in range(PAIRS_PER_STEP):
        lanes = slice(pair * LANES, (pair + 1) * LANES)
        stages = (scratch[6 * pair:6 * pair + 3], scratch[6 * pair + 3:6 * pair + 6])
        for g in range(N_BRANCHES - 1, -1, -1):
            dil = DILATIONS[g]
            per_run = CHUNK // dil // BAND
            for idx in range(BLOCKS_PER_CHUNK):
                r, n = divmod(idx, per_run)
                triple = block(lanes, g, r, n)
                own_rows = slice(idx * BAND, (idx + 1) * BAND)
                if g < N_BRANCHES - 1:
                    triple = merge(triple, [ref[own_rows, :] for ref in stages[(g + 1) % 2]])
                if g > 0:
                    inner = DILATIONS[g - 1]
                    ratio = dil // inner
                    start = (r % inner) * (CHUNK // inner) + r // inner + n * BAND * ratio
                    for ref, val in zip(stages[g % 2], triple):
                        ref[pl.ds(start, BAND, stride=ratio), :] = val
                else:
                    o_ref[own_rows, lanes] = (triple[2] / triple[1]).astype(o_ref.dtype)


def _attention(qkv, batch, seq, casts):
    chunks_per_seq = seq // CHUNK

    def here(b, j, c):
        return (b * chunks_per_seq + c, 0, 0, j)

    def tail_of_previous(b, j, c, last_block):
        return (b * chunks_per_seq + jnp.maximum(c - 1, 0), 0, last_block, j)

    in_specs, operands = [], []
    for g, dil in enumerate(DILATIONS):
        run = CHUNK // dil
        cur = pl.BlockSpec((None, dil, run, PAIRS_PER_STEP * LANES), here)
        tail = pl.BlockSpec((None, dil, BAND, PAIRS_PER_STEP * LANES),
                            lambda b, j, c, last=run // BAND - 1: tail_of_previous(b, j, c, last))
        q, k, v = qkv[3 * g:3 * g + 3]
        in_specs += [cur, cur, cur, tail, tail]
        operands += [q, k, v, k, v]
    scratch = [pltpu.VMEM((CHUNK, LANES), F32)] * (6 * PAIRS_PER_STEP)
    scratch += [pltpu.VMEM((2, BAND, 2 * BAND), F32)]
    groups = N_HEAD_PAIRS // PAIRS_PER_STEP
    steps = batch * groups * chunks_per_seq
    c_in, c_out, c_shape = _cast_plan(
        casts, steps, lambda b, j, c: (b * groups + j) * chunks_per_seq + c)
    n_in = len(in_specs)
    outs = pl.pallas_call(
        _with_casts(_attn_kernel, n_in, 1, len(casts)),
        grid=(batch, groups, chunks_per_seq),
        in_specs=in_specs + c_in,
        out_specs=[pl.BlockSpec((CHUNK, PAIRS_PER_STEP * LANES),
                                lambda b, j, c: (b * chunks_per_seq + c, j))] + c_out,
        out_shape=[jax.ShapeDtypeStruct((batch * seq, D_MODEL), BF16)] + c_shape,
        scratch_shapes=scratch,
        compiler_params=_params("arbitrary", "arbitrary", "arbitrary"),
        name="dilated_attn",
    )(*operands, *[w for w, _ in casts])
    return outs[0], outs[1:]


def kernel(x, positions, mix_norm_pre, mix_norm_post, ffn_norm_pre, ffn_norm_post, ffn_w_gate_up,
           ffn_w_down, conv_w_in, conv_w, conv_w_out, kv_norm, w_kv, w_q, w_o):
    batch, seq, d = x.shape
    tokens = batch * seq
    for window, dilation in BRANCHES:
        assert window // dilation == BAND and seq % (BAND * dilation) == 0

    def gain(g):
        return g.reshape(1, d).astype(F32)

    half = HEAD_DIM // 2
    inv_freq = ROPE_THETA ** (-jnp.arange(half, dtype=F32) / half)
    inv_freq = jnp.tile(inv_freq, LANES // half).reshape(1, LANES)
    pos = positions.reshape(tokens, 1)

    h, (w_gu0, w_dn0, w_kv16, w_q16) = _conv_mixer(
        x, gain(mix_norm_pre[0]), conv_w_in[0].astype(BF16), conv_w[0], conv_w_out[0].astype(BF16),
        gain(mix_norm_post[0]),
        casts=[(ffn_w_gate_up, 0), (ffn_w_down, 0), (w_kv[None], 0), (w_q, 0)])
    h = h.reshape(tokens, d)
    h, _ = _ffn(h, gain(ffn_norm_pre[0]), w_gu0, w_dn0, gain(ffn_norm_post[0]))

    qkv = _qkv(h, pos, inv_freq, gain(kv_norm), gain(mix_norm_pre[1]), w_kv16, w_q16)
    a, (w_gu1, w_dn1, w_o16) = _attention(
        qkv, batch, seq, casts=[(ffn_w_gate_up, 1), (ffn_w_down, 1), (w_o, 0)])
    h, _ = _ffn(h, gain(ffn_norm_pre[1]), w_gu1, w_dn1, gain(ffn_norm_post[1]),
                attn=(a, w_o16, gain(mix_norm_post[1])))
    return h.reshape(batch, seq, d)
```
